```python
import math
import jax, jax.numpy as jnp
from jax import lax
import numpy as np

D_MODEL = 4096
BATCH = 4
SEQ = 2048
DEPTH = 2
DEC_BATCH = 128
DEC_SEQ = 4
PAST_LEN = 16384
PAGE_SIZE = 128

D_MIX = D_MODEL
D_SSD = D_MIX // 2
D_S5 = D_MIX - D_SSD
SSD_HEADDIM = 64
SSD_HEADS = D_SSD // SSD_HEADDIM
SSD_GROUPS = 8
SSD_STATE = 128
SSD_CHUNK = 128
CONV_W = 4
CONV_DIM = D_SSD + 2 * SSD_GROUPS * SSD_STATE
S5_GROUP = 16
S5_GROUPS = D_S5 // S5_GROUP
S5_STATE = 64
D_IN_PROJ = D_SSD + CONV_DIM + SSD_HEADS + D_S5
N_EXPERTS = 32
TOP_K = 4
D_FF = D_MODEL
SWIGLU_LIMIT = 7.0
SWIGLU_ALPHA = 1.702
EXPERT_BLOCK = 128
N_MOD = 6
EPS = 1e-5

kernel_name = "hybrid_s5_ssd_moe_adaln_step"


def _rmsnorm(x, g):
    xf = x.astype(jnp.float32)
    y = xf * lax.rsqrt(jnp.mean(xf * xf, axis=-1, keepdims=True) + EPS)
    return (y * g.astype(jnp.float32)).astype(x.dtype)


def _gated_rmsnorm(y, z, g):
    yz = y * jax.nn.silu(z.astype(jnp.float32))
    shp = yz.shape
    yg = yz.reshape(shp[:-1] + (SSD_GROUPS, D_SSD // SSD_GROUPS))
    yg = yg * lax.rsqrt(jnp.mean(yg * yg, axis=-1, keepdims=True) + EPS)
    return yg.reshape(shp) * g.astype(jnp.float32)


def _linear_combine(e1, e2):
    a1, b1 = e1
    a2, b2 = e2
    return a1 * a2, a2 * b1 + b2


def _ssd_chunked(x, dt, a, bm, cm, h0):
    b, l, n_h, p = x.shape
    g, n = bm.shape[2], bm.shape[3]
    r = n_h // g
    q = min(SSD_CHUNK, l)
    pad = (-l) % q
    if pad:
        x = jnp.pad(x, ((0, 0), (0, pad), (0, 0), (0, 0)))
        dt = jnp.pad(dt, ((0, 0), (0, pad), (0, 0)))
        bm = jnp.pad(bm, ((0, 0), (0, pad), (0, 0), (0, 0)))
        cm = jnp.pad(cm, ((0, 0), (0, pad), (0, 0), (0, 0)))
    nc = (l + pad) // q
    xdt = (x * dt[..., None]).reshape(b, nc, q, g, r, p)
    bm = bm.reshape(b, nc, q, g, n)
    cm = cm.reshape(b, nc, q, g, n)
    da = (dt * a).reshape(b, nc, q, n_h)
    cs = jnp.moveaxis(jnp.cumsum(da, axis=2), 2, 3).reshape(b, nc, g, r, q)
    causal = jnp.tril(jnp.ones((q, q), dtype=bool))
    seg = cs[..., :, None] - cs[..., None, :]
    decay_in = jnp.exp(jnp.where(causal, seg, -jnp.inf))
    cb = jnp.einsum('bcqgn,bckgn->bcgqk', cm, bm)
    y_diag = jnp.einsum('bcgqk,bcgrqk,bckgrp->bcqgrp', cb, decay_in, xdt)
    decay_end = jnp.exp(cs[..., -1:] - cs)
    chunk_states = jnp.einsum('bckgn,bcgrk,bckgrp->bcgrpn', bm, decay_end, xdt)
    chunk_decay = jnp.exp(cs[..., -1])

    def step(h, inp):
        dec, st = inp
        return dec[..., None, None] * h + st, h

    h_final, h_in = lax.scan(step, h0.reshape(b, g, r, p, n),
                             (jnp.moveaxis(chunk_decay, 1, 0), jnp.moveaxis(chunk_states, 1, 0)))
    h_in = jnp.moveaxis(h_in, 0, 1)
    y_off = jnp.einsum('bcqgn,bcgrpn,bcgrq->bcqgrp', cm, h_in, jnp.exp(cs))
    y = (y_diag + y_off).reshape(b, nc * q, n_h, p)[:, :l]
    return y, h_final.reshape(b, n_h, p, n)


def _ssd_mixer(z, xbc, dt_raw, h0, conv_buf, conv_w, conv_b, dt_bias, a_log, d_skip, norm_g):
    b, l, _ = xbc.shape
    xp = jnp.concatenate([conv_buf.astype(xbc.dtype), xbc], axis=1)
    conv = conv_b
    for k in range(CONV_W):
        conv = conv + xp[:, k:k + l] * conv_w[k]
    new_buf = xp[:, l:].astype(jnp.float32)
    act = jax.nn.silu(conv.astype(jnp.float32))
    xs = act[..., :D_SSD].reshape(b, l, SSD_HEADS, SSD_HEADDIM)
    bm = act[..., D_SSD:D_SSD + SSD_GROUPS * SSD_STATE].reshape(b, l, SSD_GROUPS, SSD_STATE)
    cm = act[..., D_SSD + SSD_GROUPS * SSD_STATE:].reshape(b, l, SSD_GROUPS, SSD_STATE)
    dt = jax.nn.softplus(dt_raw.astype(jnp.float32) + dt_bias.astype(jnp.float32))
    a = -jnp.exp(a_log.astype(jnp.float32))
    y, h_final = _ssd_chunked(xs, dt, a, bm, cm, h0.astype(jnp.float32))
    y = y + d_skip.astype(jnp.float32)[:, None] * xs
    y = _gated_rmsnorm(y.reshape(b, l, D_SSD), z, norm_g)
    return y, h_final, new_buf


def _s5_mixer(u, s_re0, s_im0, lam_re, lam_im, log_step, b_re, b_im, c_re, c_im, d_skip, w_glu, norm_g):
    b, l, _ = u.shape
    f32 = jnp.float32
    uf = u.astype(f32).reshape(b, l, S5_GROUPS, S5_GROUP)
    lam = lax.complex(lam_re.astype(f32), lam_im.astype(f32))
    delta = jnp.exp(log_step.astype(f32))[:, None]
    lam_bar = jnp.exp(lam * delta)
    b_bar = ((lam_bar - 1.0) / lam)[..., None] * lax.complex(b_re.astype(f32), b_im.astype(f32))
    c_mat = lax.complex(c_re.astype(f32), c_im.astype(f32))
    bu = jnp.einsum('blgk,gnk->blgn', uf.astype(jnp.complex64), b_bar)
    s0 = lax.complex(s_re0.astype(f32), s_im0.astype(f32))
    bu = bu.at[:, 0].add(lam_bar * s0)
    a = jnp.broadcast_to(lam_bar, bu.shape)
    _, states = lax.associative_scan(_linear_combine, (a, bu), axis=1)
    y = jnp.real(jnp.einsum('gkn,blgn->blgk', c_mat, states)) + d_skip.astype(f32) * uf
    y = jax.nn.gelu(y).reshape(b, l, D_S5)
    y = y * jax.nn.sigmoid(y @ w_glu.astype(f32))
    y = _rmsnorm(y, norm_g)
    last = states[:, -1]
    return y, jnp.real(last), jnp.imag(last)


def _moe(h, w_router, b_router, w_gate_up, b_gate_up, w_down, b_down):
    t, d = h.shape
    logits = (h @ w_router + b_router).astype(jnp.float32)
    top_val, top_idx = lax.top_k(logits, TOP_K)
    gates = jax.nn.softmax(top_val, axis=-1)
    n_assign = t * TOP_K
    e_flat = top_idx.reshape(n_assign).astype(jnp.int32)
    tok_flat = jnp.repeat(jnp.arange(t, dtype=jnp.int32), TOP_K)
    e_s, tok_s, g_s = lax.sort((e_flat, tok_flat, gates.reshape(n_assign)), num_keys=1, is_stable=True)
    counts = jnp.zeros((N_EXPERTS,), jnp.int32).at[e_flat].add(1)
    padded = (counts + EXPERT_BLOCK - 1) // EXPERT_BLOCK * EXPERT_BLOCK
    pad_end = jnp.cumsum(padded)
    pad_start = pad_end - padded
    start = jnp.cumsum(counts) - counts
    dest = pad_start[e_s] + jnp.arange(n_assign, dtype=jnp.int32) - start[e_s]
    n_blocks = -(-n_assign // EXPERT_BLOCK) + N_EXPERTS
    n_slots = n_blocks * EXPERT_BLOCK
    slot_tok = jnp.full((n_slots,), t, jnp.int32).at[dest].set(tok_s)
    slot_gate = jnp.zeros((n_slots,), jnp.float32).at[dest].set(g_s)
    block_start = jnp.arange(n_blocks, dtype=jnp.int32) * EXPERT_BLOCK
    block_expert = jnp.minimum(jnp.searchsorted(pad_end, block_start, side='right'), N_EXPERTS - 1).astype(jnp.int32)
    h_pad = jnp.concatenate([h, jnp.zeros((1, d), h.dtype)], axis=0)

    def expert_block(args):
        tok_b, gate_b, e = args
        xb = h_pad[tok_b]
        gu = xb @ w_gate_up[e] + b_gate_up[e]
        g_lin = jnp.minimum(gu[:, :D_FF], SWIGLU_LIMIT)
        u_lin = jnp.clip(gu[:, D_FF:], -SWIGLU_LIMIT, SWIGLU_LIMIT)
        act = g_lin * jax.nn.sigmoid(SWIGLU_ALPHA * g_lin) * (u_lin + 1.0)
        out = act @ w_down[e] + b_down[e]
        return out * gate_b[:, None].astype(out.dtype)

    outs = lax.map(expert_block, (slot_tok.reshape(n_blocks, EXPERT_BLOCK),
                                  slot_gate.reshape(n_blocks, EXPERT_BLOCK), block_expert))
    return jax.ops.segment_sum(outs.reshape(n_slots, d), slot_tok, num_segments=t + 1)[:t]


def _layer(x, c, s5_re0, s5_im0, h0, conv0, W, l):
    nb, ns, d = x.shape
    mod = jax.nn.silu(c) @ W['w_ada'][l] + W['b_ada'][l]
    sh1, sc1, g1, sh2, sc2, g2 = [m[:, None, :] for m in jnp.split(mod, N_MOD, axis=-1)]
    h = _rmsnorm(x, W['norm1_g'][l]) * (1.0 + sc1) + sh1
    proj = h @ W['w_in'][l]
    o1 = D_SSD
    o2 = o1 + CONV_DIM
    o3 = o2 + SSD_HEADS
    z, xbc, dt_raw, u = proj[..., :o1], proj[..., o1:o2], proj[..., o2:o3], proj[..., o3:]
    y_ssd, h_new, conv_new = _ssd_mixer(z, xbc, dt_raw, h0, conv0, W['conv_w'][l], W['conv_b'][l],
                                        W['dt_bias'][l], W['a_log'][l], W['ssd_d'][l], W['ssd_norm_g'][l])
    y_s5, s5_re, s5_im = _s5_mixer(u, s5_re0, s5_im0, W['s5_lambda_re'][l], W['s5_lambda_im'][l],
                                   W['s5_log_step'][l], W['s5_b_re'][l], W['s5_b_im'][l],
                                   W['s5_c_re'][l], W['s5_c_im'][l], W['s5_d'][l],
                                   W['s5_w_glu'][l], W['s5_norm_g'][l])
    mix = jnp.concatenate([y_ssd, y_s5], axis=-1).astype(x.dtype) @ W['w_out'][l]
    x = x + g1 * mix
    h2 = _rmsnorm(x, W['norm2_g'][l]) * (1.0 + sc2) + sh2
    ff = _moe(h2.reshape(nb * ns, d), W['w_router'][l], W['b_router'][l], W['w_gate_up'][l],
              W['b_gate_up'][l], W['w_down'][l], W['b_down'][l]).reshape(nb, ns, d)
    x = x + g2 * ff
    return x, s5_re, s5_im, h_new, conv_new


def _trunk(x, c, s5_re, s5_im, ssd_h, conv_buf, W):
    new_re, new_im, new_h, new_conv = [], [], [], []
    for l in range(DEPTH):
        x, a_re, a_im, a_h, a_conv = _layer(x, c, s5_re[l], s5_im[l], ssd_h[l], conv_buf[l], W, l)
        new_re.append(a_re)
        new_im.append(a_im)
        new_h.append(a_h)
        new_conv.append(a_conv)
    y = _rmsnorm(x, W['final_norm_g'])
    return y, jnp.stack(new_re), jnp.stack(new_im), jnp.stack(new_h), jnp.stack(new_conv)


def setup_inputs(seed: int = 0) -> dict:
    key = jax.random.key(seed)
    ks = iter(jax.random.split(key, 48))
    f32 = jnp.float32

    def nrm(shape, scale):
        return jax.random.normal(next(ks), shape, f32) * scale

    def gain(shape):
        return 1.0 + nrm(shape, 0.02)

    inp = {}
    inp['x_prompt'] = nrm((BATCH, SEQ, D_MODEL), 1.0)
    inp['x_sample'] = nrm((DEC_BATCH, DEC_SEQ, D_MODEL), 1.0)
    inp['c_prompt'] = nrm((BATCH, D_MODEL), 1.0)
    inp['c_sample'] = nrm((DEC_BATCH, D_MODEL), 1.0)
    inp['state_s5_re'] = nrm((DEPTH, DEC_BATCH, S5_GROUPS, S5_STATE), 0.5)
    inp['state_s5_im'] = nrm((DEPTH, DEC_BATCH, S5_GROUPS, S5_STATE), 0.5)
    inp['state_ssd'] = nrm((DEPTH, DEC_BATCH, SSD_HEADS, SSD_HEADDIM, SSD_STATE), 0.5)
    inp['state_conv'] = nrm((DEPTH, DEC_BATCH, CONV_W - 1, CONV_DIM), 1.0)
    inp['norm1_g'] = gain((DEPTH, D_MODEL))
    inp['norm2_g'] = gain((DEPTH, D_MODEL))
    inp['w_ada'] = nrm((DEPTH, D_MODEL, N_MOD * D_MODEL), 0.5 * D_MODEL ** -0.5)
    inp['b_ada'] = nrm((DEPTH, N_MOD * D_MODEL), 0.02)
    inp['w_in'] = nrm((DEPTH, D_MODEL, D_IN_PROJ), D_MODEL ** -0.5)
    inp['conv_w'] = nrm((DEPTH, CONV_W, CONV_DIM), CONV_W ** -0.5)
    inp['conv_b'] = nrm((DEPTH, CONV_DIM), 0.02)
    dt0 = jnp.exp(jax.random.uniform(next(ks), (DEPTH, SSD_HEADS), f32, math.log(1e-3), math.log(1e-1)))
    inp['dt_bias'] = dt0 + jnp.log(-jnp.expm1(-dt0))
    inp['a_log'] = jnp.log(jax.random.uniform(next(ks), (DEPTH, SSD_HEADS), f32, 1.0, 16.0))
    inp['ssd_d'] = gain((DEPTH, SSD_HEADS))
    inp['ssd_norm_g'] = gain((DEPTH, D_SSD))
    inp['s5_lambda_re'] = -0.5 + nrm((DEPTH, S5_GROUPS, S5_STATE), 0.01)
    inp['s5_lambda_im'] = jnp.pi * jnp.arange(S5_STATE, dtype=f32) + nrm((DEPTH, S5_GROUPS, S5_STATE), 0.01)
    inp['s5_log_step'] = jax.random.uniform(next(ks), (DEPTH, S5_GROUPS), f32, math.log(1e-3), math.log(1e-1))
    inp['s5_b_re'] = nrm((DEPTH, S5_GROUPS, S5_STATE, S5_GROUP), 0.7 * S5_GROUP ** -0.5)
    inp['s5_b_im'] = nrm((DEPTH, S5_GROUPS, S5_STATE, S5_GROUP), 0.7 * S5_GROUP ** -0.5)
    inp['s5_c_re'] = nrm((DEPTH, S5_GROUPS, S5_GROUP, S5_STATE), 0.7 * S5_STATE ** -0.5)
    inp['s5_c_im'] = nrm((DEPTH, S5_GROUPS, S5_GROUP, S5_STATE), 0.7 * S5_STATE ** -0.5)
    inp['s5_d'] = nrm((DEPTH, S5_GROUPS, S5_GROUP), 1.0)
    inp['s5_w_glu'] = nrm((DEPTH, D_S5, D_S5), D_S5 ** -0.5)
    inp['s5_norm_g'] = gain((DEPTH, D_S5))
    inp['w_out'] = nrm((DEPTH, D_MIX, D_MODEL), D_MIX ** -0.5)
    inp['w_router'] = nrm((DEPTH, D_MODEL, N_EXPERTS), D_MODEL ** -0.5)
    inp['b_router'] = nrm((DEPTH, N_EXPERTS), 0.01)
    inp['w_gate_up'] = nrm((DEPTH, N_EXPERTS, D_MODEL, 2 * D_FF), D_MODEL ** -0.5)
    inp['b_gate_up'] = nrm((DEPTH, N_EXPERTS, 2 * D_FF), 0.02)
    inp['w_down'] = nrm((DEPTH, N_EXPERTS, D_FF, D_MODEL), D_FF ** -0.5)
    inp['b_down'] = nrm((DEPTH, N_EXPERTS, D_MODEL), 0.02)
    inp['final_norm_g'] = gain((D_MODEL,))
    return inp


def reference(x_prompt, x_sample, c_prompt, c_sample, state_s5_re, state_s5_im, state_ssd, state_conv,
              norm1_g, norm2_g, w_ada, b_ada, w_in, conv_w, conv_b, dt_bias, a_log, ssd_d, ssd_norm_g,
              s5_lambda_re, s5_lambda_im, s5_log_step, s5_b_re, s5_b_im, s5_c_re, s5_c_im, s5_d,
              s5_w_glu, s5_norm_g, w_out, w_router, b_router, w_gate_up, b_gate_up, w_down, b_down,
              final_norm_g):
    W = dict(norm1_g=norm1_g, norm2_g=norm2_g, w_ada=w_ada, b_ada=b_ada, w_in=w_in, conv_w=conv_w,
             conv_b=conv_b, dt_bias=dt_bias, a_log=a_log, ssd_d=ssd_d, ssd_norm_g=ssd_norm_g,
             s5_lambda_re=s5_lambda_re, s5_lambda_im=s5_lambda_im, s5_log_step=s5_log_step,
             s5_b_re=s5_b_re, s5_b_im=s5_b_im, s5_c_re=s5_c_re, s5_c_im=s5_c_im, s5_d=s5_d,
             s5_w_glu=s5_w_glu, s5_norm_g=s5_norm_g, w_out=w_out, w_router=w_router, b_router=b_router,
             w_gate_up=w_gate_up, b_gate_up=b_gate_up, w_down=w_down, b_down=b_down,
             final_norm_g=final_norm_g)
    bp = x_prompt.shape[0]
    f32 = jnp.float32
    p_re0 = jnp.zeros((DEPTH, bp, S5_GROUPS, S5_STATE), f32)
    p_im0 = jnp.zeros((DEPTH, bp, S5_GROUPS, S5_STATE), f32)
    p_h0 = jnp.zeros((DEPTH, bp, SSD_HEADS, SSD_HEADDIM, SSD_STATE), f32)
    p_conv0 = jnp.zeros((DEPTH, bp, CONV_W - 1, CONV_DIM), x_prompt.dtype)
    y_prompt, p_re, p_im, p_h, p_conv = _trunk(x_prompt, c_prompt, p_re0, p_im0, p_h0, p_conv0, W)
    y_sample, s_re, s_im, s_h, s_conv = _trunk(x_sample, c_sample, state_s5_re, state_s5_im,
                                               state_ssd, state_conv, W)
    return (y_prompt, y_sample, p_re, p_im, p_h, p_conv, s_re, s_im, s_h, s_conv)
```

```python
import functools
from typing import NamedTuple

import jax
import jax.numpy as jnp
from jax import lax
from jax.experimental import pallas as pl
from jax.experimental.pallas import tpu as pltpu

F32 = jnp.float32
BF16 = jnp.bfloat16
I32 = jnp.int32

LANES = 128
SUBLANES = 8
VMEM_LIMIT_MB = 60
EPS = 1e-5
N_MOD = 6
SWIGLU_LIMIT = 7.0
SWIGLU_ALPHA = 1.702
NEG_BIG = -1e30


class Cfg(NamedTuple):
    d_model: int = 4096
    batch: int = 4
    seq: int = 2048
    depth: int = 2
    dec_batch: int = 128
    dec_seq: int = 4
    d_ssd: int = 2048
    d_s5: int = 2048
    headdim: int = 64
    ssd_groups: int = 8
    ssd_state: int = 128
    conv_w: int = 4
    s5_group: int = 16
    s5_state: int = 64
    n_experts: int = 32
    top_k: int = 4
    d_ff: int = 4096
    ssd_chunk: int = 128
    s5_rows: int = 256
    moe_rows: int = 1280
    moe_sub: int = 256
    moe_gather: int = 64
    moe_tf: int = 256
    moe_td: int = 256
    mm_tm: int = 640
    mm_tn: int = 640

    @property
    def heads(self):
        return self.d_ssd // self.headdim

    @property
    def gn(self):
        return self.ssd_groups * self.ssd_state

    @property
    def conv_dim(self):
        return self.d_ssd + 2 * self.gn

    @property
    def s5_groups(self):
        return self.d_s5 // self.s5_group

    @property
    def tp(self):
        return self.batch * self.seq

    @property
    def ts(self):
        return self.dec_batch * self.dec_seq

    @property
    def t(self):
        return self.tp + self.ts

    @property
    def proj_w(self):
        return 2 * self.d_ssd + 2 * self.gn + self.d_s5 + LANES


def _pick(n, cap, mult):
    best = None
    for d in range(mult, min(n, cap) + 1, mult):
        if n % d == 0:
            best = d
    assert best is not None, (n, cap, mult)
    return best


def _cparams(sem):
    return pltpu.CompilerParams(dimension_semantics=sem, vmem_limit_bytes=VMEM_LIMIT_MB * 1024 * 1024)


def _sigmoid(x):
    return 1.0 / (1.0 + jnp.exp(-x))


def _split3(x):
    h = x.astype(BF16)
    r = x - h.astype(F32)
    m = r.astype(BF16)
    lo = (r - m.astype(F32)).astype(BF16)
    return h, m, lo


def _dot(a, b):
    return jnp.dot(a, b, preferred_element_type=F32)


def _mod_block_index(cfg, i, tm):
    return jnp.where(i < cfg.tp // tm, (i * tm) // cfg.seq, cfg.batch)


def _ada_body(c_ref, w_ref, b_ref, o_ref):
    c = c_ref[...]
    a = (c * _sigmoid(c)).astype(BF16)
    o_ref[...] = _dot(a, w_ref[...].astype(BF16)) + b_ref[...]


def _ada_call(c_all, w_ada, b_ada):
    depth, d, n6 = w_ada.shape
    rc = c_all.shape[0]
    tn = _pick(n6, 512, LANES)
    return pl.pallas_call(
        _ada_body,
        out_shape=jax.ShapeDtypeStruct((depth, rc, n6), F32),
        grid=(depth, n6 // tn),
        in_specs=[
            pl.BlockSpec((rc, d), lambda l, n: (0, 0)),
            pl.BlockSpec((None, d, tn), lambda l, n: (l, 0, n)),
            pl.BlockSpec((None, 1, tn), lambda l, n: (l, 0, n)),
        ],
        out_specs=pl.BlockSpec((None, rc, tn), lambda l, n: (l, 0, n)),
        compiler_params=_cparams(("arbitrary", "arbitrary")),
        name="ada",
    )(c_all, w_ada, b_ada.reshape(depth, 1, n6))


def _rms_mod(x, g, sc, sh, nrep, nb):
    d = x.shape[-1]
    y = x * lax.rsqrt(jnp.mean(x * x, axis=-1, keepdims=True) + EPS) * g
    y3 = y.reshape(nrep, nb, d)
    h = y3 * (1.0 + sc[None]) + sh[None]
    return h.reshape(nrep * nb, d)


def _norm_mod_body(x_ref, g_ref, sh_ref, sc_ref, o_ref, *, nrep, nb):
    h = _rms_mod(x_ref[...], g_ref[...], sc_ref[...], sh_ref[...], nrep, nb)
    o_ref[...] = h.astype(o_ref.dtype)


def _norm_mod_call(cfg, x, g, mod_rows, layer, shift_chunk, scale_chunk):
    t, d = x.shape
    nb = cfg.dec_batch
    tm = cfg.ts
    bidx = functools.partial(_mod_block_index, cfg, tm=tm)
    return pl.pallas_call(
        functools.partial(_norm_mod_body, nrep=tm // nb, nb=nb),
        out_shape=jax.ShapeDtypeStruct((t, d), BF16),
        grid=(t // tm,),
        in_specs=[
            pl.BlockSpec((tm, d), lambda i: (i, 0)),
            pl.BlockSpec((1, d), lambda i: (0, 0)),
            pl.BlockSpec((None, nb, d), lambda i: (layer, bidx(i), shift_chunk)),
            pl.BlockSpec((None, nb, d), lambda i: (layer, bidx(i), scale_chunk)),
        ],
        out_specs=pl.BlockSpec((tm, d), lambda i: (i, 0)),
        compiler_params=_cparams(("arbitrary",)),
        name="norm_mod",
    )(x, g.reshape(1, d), mod_rows, mod_rows)


def _mm_body(a_ref, w_ref, o_ref, wbf_ref):
    @pl.when(pl.program_id(1) == 0)
    def _():
        wbf_ref[...] = w_ref[...].astype(BF16)

    o_ref[...] = _dot(a_ref[...], wbf_ref[...]).astype(o_ref.dtype)


def _mm_call(cfg, a, w):
    m, k = a.shape
    n = w.shape[1]
    tm = _pick(m, cfg.mm_tm, 16)
    tn = _pick(n, cfg.mm_tn, LANES)
    return pl.pallas_call(
        _mm_body,
        out_shape=jax.ShapeDtypeStruct((m, n), F32),
        grid=(n // tn, m // tm),
        in_specs=[
            pl.BlockSpec((tm, k), lambda j, i: (i, 0)),
            pl.BlockSpec((k, tn), lambda j, i: (0, j)),
        ],
        out_specs=pl.BlockSpec((tm, tn), lambda j, i: (i, j)),
        scratch_shapes=[pltpu.VMEM((k, tn), BF16)],
        compiler_params=_cparams(("arbitrary", "arbitrary")),
        name="in_proj",
    )(a, w)


def _ssd_body(*refs, cfg, q_in, l_real, has_init):
    if has_init:
        (dsk_ref, z_ref, xs_ref, bc_ref, dt_ref, cw_ref, cb_ref, dtb_ref, alog_ref, ng_ref,
         h0_ref, tail0_ref, y_ref, hout_ref, cout_ref, tail_ref, ext_ref, yt_ref) = refs
    else:
        (dsk_ref, z_ref, xs_ref, bc_ref, dt_ref, cw_ref, cb_ref, dtb_ref, alog_ref, ng_ref,
         y_ref, hout_ref, cout_ref, tail_ref, ext_ref, yt_ref) = refs
    q = cfg.ssd_chunk
    p = cfg.headdim
    n = cfg.ssd_state
    g_cnt = cfg.ssd_groups
    heads = cfg.heads
    hpg = heads // g_cnt
    d_ssd = cfg.d_ssd
    cdim = cfg.conv_dim
    c = pl.program_id(1)

    @pl.when(c == 0)
    def _():
        if has_init:
            hout_ref[...] = h0_ref[...]
            tail_ref[...] = jnp.concatenate(
                [jnp.zeros((SUBLANES - cfg.conv_w + 1, cdim), F32), tail0_ref[...]], axis=0)
        else:
            hout_ref[...] = jnp.zeros(hout_ref.shape, F32)
            tail_ref[...] = jnp.zeros(tail_ref.shape, F32)

    raw = jnp.concatenate([xs_ref[...], bc_ref[...]], axis=1)
    if q_in < q:
        raw = jnp.concatenate([raw, jnp.zeros((q - q_in, cdim), F32)], axis=0)
    ext_ref[0:SUBLANES, :] = tail_ref[...]
    ext_ref[SUBLANES:SUBLANES + q, :] = raw
    conv = jnp.broadcast_to(cb_ref[...], (q, cdim))
    for k in range(cfg.conv_w):
        conv = conv + ext_ref[pl.ds(SUBLANES - cfg.conv_w + 1 + k, q), :] * cw_ref[k:k + 1, :]
    tail_ref[...] = ext_ref[pl.ds(q, SUBLANES), :]
    cout_ref[...] = ext_ref[pl.ds(SUBLANES + l_real - (cfg.conv_w - 1), cfg.conv_w - 1), :]

    act = conv * _sigmoid(conv)
    xs_a = act[:, :d_ssd]
    bm = act[:, d_ssd:d_ssd + cfg.gn]
    cm = act[:, d_ssd + cfg.gn:]

    dtv = dt_ref[...] + dtb_ref[...]
    if q_in < q:
        dtv = jnp.concatenate([dtv, jnp.zeros((q - q_in, LANES), F32)], axis=0)
    dtv = jnp.maximum(dtv, 0.0) + jnp.log(1.0 + jnp.exp(-jnp.abs(dtv)))
    if l_real < q:
        rows = lax.broadcasted_iota(I32, (q, LANES), 0)
        dtv = jnp.where(rows < l_real, dtv, 0.0)
    da = dtv * (-jnp.exp(alog_ref[...]))
    ri = lax.broadcasted_iota(I32, (q, q), 0)
    ci = lax.broadcasted_iota(I32, (q, q), 1)
    tril = (ri >= ci).astype(BF16)
    d_h, d_m, d_l = _split3(da)
    cs = _dot(tril, d_h) + _dot(tril, d_m) + _dot(tril, d_l)
    cs_t = cs.T
    dt_t = dtv.T
    xs_t = xs_a.T
    causal_t = ci >= ri

    for g in range(g_cnt):
        bm_g = bm[:, g * n:(g + 1) * n].astype(BF16)
        cm_t = cm[:, g * n:(g + 1) * n].T.astype(BF16)
        cb_t = _dot(bm_g, cm_t)
        for hh in range(hpg):
            h = g * hpg + hh
            cs_col = cs[:, h:h + 1]
            cs_row = cs_t[h:h + 1, :]
            l_t = cb_t * jnp.exp(jnp.where(causal_t, cs_row - cs_col, -jnp.inf))
            x_t = xs_t[h * p:(h + 1) * p, :]
            xdt_t = x_t * dt_t[h:h + 1, :]
            y_diag = _dot(xdt_t.astype(BF16), l_t.astype(BF16))
            hs = hout_ref[h]
            y_off = _dot(hs.astype(BF16), cm_t) * jnp.exp(cs_row)
            cs_last = cs_row[:, q - 1:q]
            xw_t = xdt_t * jnp.exp(cs_last - cs_row)
            hout_ref[h] = jnp.exp(cs_last) * hs + _dot(xw_t.astype(BF16), bm_g)
            yt_ref[h * p:(h + 1) * p, :] = y_diag + y_off + dsk_ref[h] * x_t

    y = yt_ref[...].T
    z = z_ref[...]
    if q_in < q:
        y = y[:q_in]
    yz = y * (z * _sigmoid(z))
    gw = d_ssd // g_cnt
    parts = []
    for g in range(g_cnt):
        seg = yz[:, g * gw:(g + 1) * gw]
        parts.append(seg * lax.rsqrt(jnp.mean(seg * seg, axis=-1, keepdims=True) + EPS))
    y_ref[...] = (jnp.concatenate(parts, axis=1) * ng_ref[...]).astype(y_ref.dtype)


def _ssd_call(cfg, proj, params, init=None):
    q = cfg.ssd_chunk
    d_ssd, gn2, cdim = cfg.d_ssd, 2 * cfg.gn, cfg.conv_dim
    heads, p, n = cfg.heads, cfg.headdim, cfg.ssd_state
    assert (2 * d_ssd) % gn2 == 0 and heads <= LANES
    bc_blk = (2 * d_ssd) // gn2
    dt_blk = (2 * d_ssd + gn2 + cfg.d_s5) // LANES
    conv_w, conv_b, dt_bias, a_log, d_skip, norm_g = params
    pad = LANES - heads
    small = [
        conv_w, conv_b.reshape(1, cdim),
        jnp.pad(dt_bias, (0, pad)).reshape(1, LANES), jnp.pad(a_log, (0, pad)).reshape(1, LANES),
        norm_g.reshape(1, d_ssd),
    ]
    small_specs = [
        pl.BlockSpec((cfg.conv_w, cdim), lambda b, c: (0, 0)),
        pl.BlockSpec((1, cdim), lambda b, c: (0, 0)),
        pl.BlockSpec((1, LANES), lambda b, c: (0, 0)),
        pl.BlockSpec((1, LANES), lambda b, c: (0, 0)),
        pl.BlockSpec((1, d_ssd), lambda b, c: (0, 0)),
    ]
    smem = pl.BlockSpec(memory_space=pltpu.SMEM)
    if init is None:
        nb, nc, q_in, l_real = cfg.batch, cfg.seq // q, q, q
        row_specs = [
            pl.BlockSpec((q, d_ssd), lambda b, c: (b * nc + c, 0)),
            pl.BlockSpec((q, d_ssd), lambda b, c: (b * nc + c, 1)),
            pl.BlockSpec((q, gn2), lambda b, c: (b * nc + c, bc_blk)),
            pl.BlockSpec((q, LANES), lambda b, c: (b * nc + c, dt_blk)),
        ]
        init_args, init_specs = [], []
        y_shape = jax.ShapeDtypeStruct((cfg.tp, d_ssd), BF16)
        y_spec = pl.BlockSpec((q, d_ssd), lambda b, c: (b * nc + c, 0))
    else:
        nb, nc, q_in, l_real = cfg.dec_batch, 1, SUBLANES, cfg.dec_seq
        row_specs = [
            pl.BlockSpec((None, q_in, d_ssd), lambda b, c: (b, 0, 0)),
            pl.BlockSpec((None, q_in, d_ssd), lambda b, c: (b, 0, 1)),
            pl.BlockSpec((None, q_in, gn2), lambda b, c: (b, 0, bc_blk)),
            pl.BlockSpec((None, q_in, LANES), lambda b, c: (b, 0, dt_blk)),
        ]
        init_args = list(init)
        init_specs = [
            pl.BlockSpec((None, heads, p, n), lambda b, c: (b, 0, 0, 0)),
            pl.BlockSpec((None, cfg.conv_w - 1, cdim), lambda b, c: (b, 0, 0)),
        ]
        y_shape = jax.ShapeDtypeStruct((nb, q_in, d_ssd), BF16)
        y_spec = pl.BlockSpec((None, q_in, d_ssd), lambda b, c: (b, 0, 0))
    return pl.pallas_call(
        functools.partial(_ssd_body, cfg=cfg, q_in=q_in, l_real=l_real, has_init=init is not None),
        out_shape=[
            y_shape,
            jax.ShapeDtypeStruct((nb, heads, p, n), F32),
            jax.ShapeDtypeStruct((nb, cfg.conv_w - 1, cdim), F32),
        ],
        grid=(nb, nc),
        in_specs=[smem] + row_specs + small_specs + init_specs,
        out_specs=[
            y_spec,
            pl.BlockSpec((None, heads, p, n), lambda b, c: (b, 0, 0, 0)),
            pl.BlockSpec((None, cfg.conv_w - 1, cdim), lambda b, c: (b, 0, 0)),
        ],
        scratch_shapes=[
            pltpu.VMEM((SUBLANES, cdim), F32),
            pltpu.VMEM((q + 2 * SUBLANES, cdim), F32),
            pltpu.VMEM((d_ssd, q), F32),
        ],
        compiler_params=_cparams(("arbitrary", "arbitrary")),
        name="ssd_sample" if init is not None else "ssd_prompt",
    )(d_skip, proj, proj, proj, proj, *small, *init_args)


S5_TILE_GROUPS = LANES // 16


def _s5_tables(cfg, lam_re, lam_im, log_step, b_re, b_im, c_re, c_im):
    g_cnt, n, k = cfg.s5_groups, cfg.s5_state, cfg.s5_group
    tg = LANES // k
    n_lt = g_cnt // tg
    lam = lax.complex(lam_re, lam_im)
    delta = jnp.exp(log_step)[:, None]
    lam_bar = jnp.exp(lam * delta)
    b_bar = ((lam_bar - 1.0) / lam)[..., None] * lax.complex(b_re, b_im)
    eye = jnp.eye(tg, dtype=F32)

    def blockdiag(m):
        a, b = m.shape[1], m.shape[2]
        m4 = m.reshape(n_lt, tg, a, b)
        return (m4[:, :, :, None, :] * eye[None, :, None, :, None]).reshape(n_lt, tg * a, tg * b)

    bt = jnp.swapaxes(b_bar, 1, 2)
    bdb = jnp.concatenate([blockdiag(jnp.real(bt)), blockdiag(jnp.imag(bt))], axis=-1).astype(BF16)
    ct_re = jnp.swapaxes(c_re, 1, 2)
    ct_im = jnp.swapaxes(c_im, 1, 2)
    bdc = jnp.concatenate([blockdiag(ct_re), blockdiag(-ct_im)], axis=1).astype(BF16)
    pows = []
    cur = lam_bar
    for _ in range(SUBLANES):
        flat = cur.reshape(n_lt, tg * n)
        pows.append(jnp.concatenate([jnp.real(flat), jnp.imag(flat)], axis=-1))
        cur = cur * cur
    pw = jnp.stack(pows, axis=1)
    return bdb, bdc, pw


def _gelu(y):
    return 0.5 * y * (1.0 + jnp.tanh(0.7978845608028654 * (y + 0.044715 * y * y * y)))


def _s5_prompt_body(u_ref, bdb_ref, bdc_ref, pw_ref, d_ref, y_ref, sre_ref, sim_ref, car_ref, *, rows, half):
    t = pl.program_id(2)

    @pl.when(t == 0)
    def _():
        car_ref[...] = jnp.zeros(car_ref.shape, F32)

    u = u_ref[...]
    bu = _dot(u.astype(BF16), bdb_ref[...])
    xr = bu[:, :half]
    xi = bu[:, half:]
    ridx = lax.broadcasted_iota(I32, (rows, half), 0)
    lr = pw_ref[0:1, :half]
    li = pw_ref[0:1, half:]
    cr = car_ref[0:1, :half]
    cim = car_ref[0:1, half:]
    first = ridx == 0
    xr = xr + jnp.where(first, lr * cr - li * cim, 0.0)
    xi = xi + jnp.where(first, lr * cim + li * cr, 0.0)
    step = 1
    k = 0
    while step < rows:
        pr = pw_ref[k:k + 1, :half]
        pi = pw_ref[k:k + 1, half:]
        keep = ridx >= step
        sr = jnp.where(keep, pltpu.roll(xr, step, 0), 0.0)
        si = jnp.where(keep, pltpu.roll(xi, step, 0), 0.0)
        xr, xi = xr + pr * sr - pi * si, xi + pr * si + pi * sr
        step *= 2
        k += 1
    car_ref[0:1, :half] = xr[rows - 1:rows, :]
    car_ref[0:1, half:] = xi[rows - 1:rows, :]
    sre_ref[...] = xr[rows - 1:rows, :]
    sim_ref[...] = xi[rows - 1:rows, :]
    s_cat = jnp.concatenate([xr, xi], axis=1).astype(BF16)
    y = _dot(s_cat, bdc_ref[...]) + d_ref[...] * u
    y_ref[...] = _gelu(y)


def _s5_prompt_call(cfg, proj, tables, d_skip):
    bdb, bdc, pw = tables
    n_lt = bdb.shape[0]
    half = bdb.shape[2] // 2
    rows = cfg.s5_rows
    assert rows <= 2 ** SUBLANES and cfg.seq % rows == 0
    nt = cfg.seq // rows
    u_blk = (2 * cfg.d_ssd + 2 * cfg.gn) // LANES
    return pl.pallas_call(
        functools.partial(_s5_prompt_body, rows=rows, half=half),
        out_shape=[
            jax.ShapeDtypeStruct((cfg.tp, cfg.d_s5), F32),
            jax.ShapeDtypeStruct((cfg.batch, n_lt, 1, half), F32),
            jax.ShapeDtypeStruct((cfg.batch, n_lt, 1, half), F32),
        ],
        grid=(cfg.batch, n_lt, nt),
        in_specs=[
            pl.BlockSpec((rows, LANES), lambda b, lt, t: (b * nt + t, u_blk + lt)),
            pl.BlockSpec((None, LANES, 2 * half), lambda b, lt, t: (lt, 0, 0)),
            pl.BlockSpec((None, 2 * half, LANES), lambda b, lt, t: (lt, 0, 0)),
            pl.BlockSpec((None, SUBLANES, 2 * half), lambda b, lt, t: (lt, 0, 0)),
            pl.BlockSpec((1, LANES), lambda b, lt, t: (0, lt)),
        ],
        out_specs=[
            pl.BlockSpec((rows, LANES), lambda b, lt, t: (b * nt + t, lt)),
            pl.BlockSpec((None, None, 1, half), lambda b, lt, t: (b, lt, 0, 0)),
            pl.BlockSpec((None, None, 1, half), lambda b, lt, t: (b, lt, 0, 0)),
        ],
        scratch_shapes=[pltpu.VMEM((SUBLANES, 2 * half), F32)],
        compiler_params=_cparams(("arbitrary", "arbitrary", "arbitrary")),
        name="s5_prompt",
    )(proj, bdb, bdc, pw, d_skip.reshape(1, cfg.d_s5))


def _s5_sample_body(u_ref, bdb_ref, bdc_ref, pw_ref, d_ref, s0r_ref, s0i_ref, y_ref, sre_ref, sim_ref,
                    *, nb, nseq, half):
    u = u_ref[...]
    bu = _dot(u.astype(BF16), bdb_ref[...])
    lr = pw_ref[0:1, :half]
    li = pw_ref[0:1, half:]
    sr = s0r_ref[...]
    si = s0i_ref[...]
    st_r, st_i = [], []
    for s in range(nseq):
        br = bu[s * nb:(s + 1) * nb, :half]
        bi = bu[s * nb:(s + 1) * nb, half:]
        sr, si = lr * sr - li * si + br, lr * si + li * sr + bi
        st_r.append(sr)
        st_i.append(si)
    sre_ref[...] = sr
    sim_ref[...] = si
    s_cat = jnp.concatenate([jnp.concatenate(st_r, axis=0), jnp.concatenate(st_i, axis=0)], axis=1)
    y = _dot(s_cat.astype(BF16), bdc_ref[...]) + d_ref[...] * u
    y_ref[...] = _gelu(y)


def _s5_sample_call(cfg, proj, tables, d_skip, s_re0, s_im0):
    bdb, bdc, pw = tables
    n_lt = bdb.shape[0]
    half = bdb.shape[2] // 2
    nb, nseq = cfg.dec_batch, cfg.dec_seq
    assert cfg.tp % cfg.ts == 0
    u_blk = (2 * cfg.d_ssd + 2 * cfg.gn) // LANES
    row_blk = cfg.tp // cfg.ts
    gsz = cfg.s5_groups * cfg.s5_state
    return pl.pallas_call(
        functools.partial(_s5_sample_body, nb=nb, nseq=nseq, half=half),
        out_shape=[
            jax.ShapeDtypeStruct((cfg.ts, cfg.d_s5), F32),
            jax.ShapeDtypeStruct((nb, gsz), F32),
            jax.ShapeDtypeStruct((nb, gsz), F32),
        ],
        grid=(n_lt,),
        in_specs=[
            pl.BlockSpec((cfg.ts, LANES), lambda lt: (row_blk, u_blk + lt)),
            pl.BlockSpec((None, LANES, 2 * half), lambda lt: (lt, 0, 0)),
            pl.BlockSpec((None, 2 * half, LANES), lambda lt: (lt, 0, 0)),
            pl.BlockSpec((None, SUBLANES, 2 * half), lambda lt: (lt, 0, 0)),
            pl.BlockSpec((1, LANES), lambda lt: (0, lt)),
            pl.BlockSpec((nb, half), lambda lt: (0, lt)),
            pl.BlockSpec((nb, half), lambda lt: (0, lt)),
        ],
        out_specs=[
            pl.BlockSpec((cfg.ts, LANES), lambda lt: (0, lt)),
            pl.BlockSpec((nb, half), lambda lt: (0, lt)),
            pl.BlockSpec((nb, half), lambda lt: (0, lt)),
        ],
        compiler_params=_cparams(("arbitrary",)),
        name="s5_sample",
    )(proj, bdb, bdc, pw, d_skip.reshape(1, cfg.d_s5), s_re0.reshape(nb, gsz), s_im0.reshape(nb, gsz))


def _glu_norm_body(y_ref, w_ref, g_ref, o_ref, wbf_ref):
    @pl.when(pl.program_id(0) == 0)
    def _():
        wbf_ref[...] = w_ref[...].astype(BF16)

    y = y_ref[...]
    v = y * _sigmoid(_dot(y.astype(BF16), wbf_ref[...]))
    v = v * lax.rsqrt(jnp.mean(v * v, axis=-1, keepdims=True) + EPS) * g_ref[...]
    o_ref[...] = v.astype(o_ref.dtype)


def _glu_norm_call(cfg, y, w_glu, g):
    t, d = y.shape
    tm = _pick(t, cfg.mm_tm, 16)
    return pl.pallas_call(
        _glu_norm_body,
        out_shape=jax.ShapeDtypeStruct((t, d), BF16),
        grid=(t // tm,),
        in_specs=[
            pl.BlockSpec((tm, d), lambda i: (i, 0)),
            pl.BlockSpec((d, d), lambda i: (0, 0)),
            pl.BlockSpec((1, d), lambda i: (0, 0)),
        ],
        out_specs=pl.BlockSpec((tm, d), lambda i: (i, 0)),
        scratch_shapes=[pltpu.VMEM((d, d), BF16)],
        compiler_params=_cparams(("arbitrary",)),
        name="s5_glu_norm",
    )(y, w_glu, g.reshape(1, d))


def _outproj_body(ya_ref, yb_ref, w_ref, x_ref, g1_ref, o_ref, wbf_ref, *, d_a, nrep, nb):
    @pl.when(pl.program_id(1) == 0)
    def _():
        wbf_ref[...] = w_ref[...].astype(BF16)

    mix = _dot(ya_ref[...], wbf_ref[:d_a, :]) + _dot(yb_ref[...], wbf_ref[d_a:, :])
    tn = mix.shape[1]
    gated = (mix.reshape(nrep, nb, tn) * g1_ref[...][None]).reshape(nrep * nb, tn)
    o_ref[...] = x_ref[...] + gated


def _outproj_call(cfg, ya, yb, w_out, x, mod_rows, layer, gate_chunk):
    t, d = x.shape
    d_a, d_b = ya.shape[1], yb.shape[1]
    nb = cfg.dec_batch
    tm = cfg.ts
    tn = _pick(d, 512, LANES)
    bidx = functools.partial(_mod_block_index, cfg, tm=tm)
    nblk = d // tn
    return pl.pallas_call(
        functools.partial(_outproj_body, d_a=d_a, nrep=tm // nb, nb=nb),
        out_shape=jax.ShapeDtypeStruct((t, d), F32),
        grid=(nblk, t // tm),
        in_specs=[
            pl.BlockSpec((tm, d_a), lambda j, i: (i, 0)),
            pl.BlockSpec((tm, d_b), lambda j, i: (i, 0)),
            pl.BlockSpec((d_a + d_b, tn), lambda j, i: (0, j)),
            pl.BlockSpec((tm, tn), lambda j, i: (i, j)),
            pl.BlockSpec((None, nb, tn), lambda j, i: (layer, bidx(i), gate_chunk * nblk + j)),
        ],
        out_specs=pl.BlockSpec((tm, tn), lambda j, i: (i, j)),
        scratch_shapes=[pltpu.VMEM((d_a + d_b, tn), BF16)],
        compiler_params=_cparams(("arbitrary", "arbitrary")),
        name="out_proj",
    )(ya, yb, w_out, x, mod_rows)


def _router_body(x_ref, g_ref, sh_ref, sc_ref, wr_ref, br_ref, h_ref, mi_ref, mf_ref, cnt_ref,
                 whi_ref, wlo_ref, car_ref, *, nrep, nb, top_k):
    i = pl.program_id(0)

    @pl.when(i == 0)
    def _():
        w = wr_ref[...]
        hi = w.astype(BF16)
        whi_ref[...] = hi
        wlo_ref[...] = (w - hi.astype(F32)).astype(BF16)
        car_ref[...] = jnp.zeros(car_ref.shape, F32)

    h = _rms_mod(x_ref[...], g_ref[...], sc_ref[...], sh_ref[...], nrep, nb)
    h_ref[...] = h
    tm = h.shape[0]
    hi = h.astype(BF16)
    lo = (h - hi.astype(F32)).astype(BF16)
    logits = _dot(hi, whi_ref[...]) + (_dot(hi, wlo_ref[...]) + _dot(lo, whi_ref[...])) + br_ref[...]
    lane_i = lax.broadcasted_iota(I32, (tm, LANES), 1)
    lane = lane_i.astype(F32)
    vals = logits
    idxs, tops = [], []
    for _ in range(top_k):
        m = jnp.max(vals, axis=-1, keepdims=True)
        idx = jnp.min(jnp.where(vals == m, lane, float(LANES)), axis=-1, keepdims=True)
        idxs.append(idx)
        tops.append(m)
        vals = jnp.where(lane == idx, -jnp.inf, vals)
    exps = [jnp.exp(v - tops[0]) for v in tops]
    denom = exps[0]
    for e in exps[1:]:
        denom = denom + e
    onehot = jnp.zeros((tm, LANES), F32)
    for idx in idxs:
        onehot = onehot + (lane == idx).astype(F32)
    ri = lax.broadcasted_iota(I32, (tm, tm), 0)
    ci = lax.broadcasted_iota(I32, (tm, tm), 1)
    before = _dot((ri > ci).astype(BF16), onehot.astype(BF16)) + car_ref[0:1, :]
    car_ref[0:1, :] = car_ref[0:1, :] + jnp.sum(onehot, axis=0, keepdims=True)
    cnt_ref[...] = car_ref[0:1, :]
    mi = jnp.zeros((tm, LANES), I32)
    mf = jnp.zeros((tm, LANES), F32)
    for k in range(top_k):
        rank = jnp.sum(jnp.where(lane == idxs[k], before, 0.0), axis=-1, keepdims=True).astype(I32)
        mi = jnp.where(lane_i == k, idxs[k].astype(I32), mi)
        mi = jnp.where(lane_i == top_k + k, rank, mi)
        mf = jnp.where(lane_i == k, exps[k] / denom, mf)
    mi_ref[...] = mi
    mf_ref[...] = mf


def _router_call(cfg, x, g, mod_rows, layer, shift_chunk, scale_chunk, w_router, b_router):
    t, d = x.shape
    nb = cfg.dec_batch
    tm = cfg.ts
    e = cfg.n_experts
    assert e <= LANES and 2 * cfg.top_k <= LANES
    bidx = functools.partial(_mod_block_index, cfg, tm=tm)
    wr = jnp.pad(w_router, ((0, 0), (0, LANES - e)))
    br = jnp.pad(b_router, (0, LANES - e), constant_values=NEG_BIG).reshape(1, LANES)
    return pl.pallas_call(
        functools.partial(_router_body, nrep=tm // nb, nb=nb, top_k=cfg.top_k),
        out_shape=[
            jax.ShapeDtypeStruct((t, d), F32),
            jax.ShapeDtypeStruct((t, LANES), I32),
            jax.ShapeDtypeStruct((t, LANES), F32),
            jax.ShapeDtypeStruct((1, LANES), F32),
        ],
        grid=(t // tm,),
        in_specs=[
            pl.BlockSpec((tm, d), lambda i: (i, 0)),
            pl.BlockSpec((1, d), lambda i: (0, 0)),
            pl.BlockSpec((None, nb, d), lambda i: (layer, bidx(i), shift_chunk)),
            pl.BlockSpec((None, nb, d), lambda i: (layer, bidx(i), scale_chunk)),
            pl.BlockSpec((d, LANES), lambda i: (0, 0)),
            pl.BlockSpec((1, LANES), lambda i: (0, 0)),
        ],
        out_specs=[
            pl.BlockSpec((tm, d), lambda i: (i, 0)),
            pl.BlockSpec((tm, LANES), lambda i: (i, 0)),
            pl.BlockSpec((tm, LANES), lambda i: (i, 0)),
            pl.BlockSpec((1, LANES), lambda i: (0, 0)),
        ],
        scratch_shapes=[
            pltpu.VMEM((d, LANES), BF16),
            pltpu.VMEM((d, LANES), BF16),
            pltpu.VMEM((SUBLANES, LANES), F32),
        ],
        compiler_params=_cparams(("arbitrary",)),
        name="router",
    )(x, g.reshape(1, d), mod_rows, mod_rows, wr, br)


def _moe_body(ie_ref, nsub_ref, nit_ref, tok_ref, h_hbm, wg_ref, wu_ref, wd_ref, bg_ref, bu_ref, bd_ref,
              o_ref, stage_ref, xb_ref, act_ref, wbf_ref, sem, *, cfg, n_f):
    i = pl.program_id(0)
    j = pl.program_id(1)
    sb, gb = cfg.moe_sub, cfg.moe_gather
    tf, td, d_ff, d_in = cfg.moe_tf, cfg.moe_td, cfg.d_ff, cfg.d_model
    per = sb // gb

    @pl.when(i < nit_ref[0])
    def _():
        nsub = nsub_ref[i]

        @pl.when(j == 0)
        def _():
            n_blk = nsub * per

            def issue(blk, slot):
                def one(r, carry):
                    tok = tok_ref[0, blk * gb + r]
                    pltpu.make_async_copy(h_hbm.at[pl.ds(tok, 1), :], stage_ref.at[slot, pl.ds(r, 1), :],
                                          sem.at[slot]).start()
                    return carry
                lax.fori_loop(0, gb, one, 0)

            issue(0, 0)

            def blk_step(blk, carry):
                slot = blk % 2

                @pl.when(blk + 1 < n_blk)
                def _():
                    issue(blk + 1, 1 - slot)

                pltpu.make_async_copy(h_hbm.at[pl.ds(0, gb), :], stage_ref.at[slot], sem.at[slot]).wait()
                xb_ref[pl.ds(pl.multiple_of(blk * gb, gb), gb), :] = stage_ref[slot].astype(BF16)
                return carry

            lax.fori_loop(0, n_blk, blk_step, 0)

        @pl.when(j < n_f)
        def _():
            wbf_ref[0:d_in, 0:tf] = wg_ref[...].astype(BF16)
            wbf_ref[0:d_in, tf:2 * tf] = wu_ref[...].astype(BF16)

            def sub(s, carry):
                r0 = pl.multiple_of(s * sb, sb)
                gu = _dot(xb_ref[pl.ds(r0, sb), :], wbf_ref[0:d_in, 0:2 * tf])
                gl = jnp.minimum(gu[:, :tf] + bg_ref[...], SWIGLU_LIMIT)
                ul = jnp.clip(gu[:, tf:] + bu_ref[...], -SWIGLU_LIMIT, SWIGLU_LIMIT)
                a = gl * _sigmoid(SWIGLU_ALPHA * gl) * (ul + 1.0)
                act_ref[j, pl.ds(r0, sb), :] = a.astype(BF16)
                return carry

            lax.fori_loop(0, nsub, sub, 0)

        @pl.when(j >= n_f)
        def _():
            wbf_ref[0:d_ff, 0:td] = wd_ref[...].astype(BF16)

            def sub(s, carry):
                r0 = pl.multiple_of(s * sb, sb)
                a = jnp.concatenate([act_ref[jj, pl.ds(r0, sb), :] for jj in range(n_f)], axis=1)
                o_ref[pl.ds(r0, sb), :] = _dot(a, wbf_ref[0:d_ff, 0:td]) + bd_ref[...]
                return carry

            lax.fori_loop(0, nsub, sub, 0)

            def clear(s, carry):
                o_ref[pl.ds(pl.multiple_of(s * sb, sb), sb), :] = jnp.zeros((sb, td), F32)
                return carry

            lax.fori_loop(nsub, cfg.moe_rows // sb, clear, 0)

    @pl.when(jnp.logical_and(i >= nit_ref[0], j >= n_f))
    def _():
        o_ref[...] = jnp.zeros(o_ref.shape, F32)


def _moe_call(cfg, layer, h2, tok_tbl, item_e, item_nsub, n_items, w_gate_up, b_gate_up, w_down, b_down):
    t, d = h2.shape
    tm, tf, td, d_ff = cfg.moe_rows, cfg.moe_tf, cfg.moe_td, cfg.d_ff
    ni = tok_tbl.shape[0]
    n_f, n_d = d_ff // tf, d // td
    depth, e = w_gate_up.shape[0], w_gate_up.shape[1]
    steps = n_f + n_d
    assert tm % cfg.moe_sub == 0 and cfg.moe_sub % cfg.moe_gather == 0

    def eff(i, j, nit):
        valid = i < nit[0]
        return jnp.where(valid, i, nit[0] - 1), jnp.where(valid, j, steps - 1)

    def wg_map(i, j, ie, ns, nit):
        i2, j2 = eff(i, j, nit)
        return (layer, ie[i2], 0, jnp.minimum(j2, n_f - 1))

    def wu_map(i, j, ie, ns, nit):
        i2, j2 = eff(i, j, nit)
        return (layer, ie[i2], 0, n_f + jnp.minimum(j2, n_f - 1))

    def wd_map(i, j, ie, ns, nit):
        i2, j2 = eff(i, j, nit)
        return (layer, ie[i2], 0, jnp.maximum(j2 - n_f, 0))

    def o_map(i, j, ie, ns, nit):
        return (i, jnp.maximum(j - n_f, 0))

    grid_spec = pltpu.PrefetchScalarGridSpec(
        num_scalar_prefetch=3,
        grid=(ni, steps),
        in_specs=[
            pl.BlockSpec((None, 1, tm), lambda i, j, ie, ns, nit: (jnp.minimum(i, nit[0] - 1), 0, 0),
                         memory_space=pltpu.SMEM),
            pl.BlockSpec(memory_space=pl.ANY),
            pl.BlockSpec((None, None, d, tf), wg_map),
            pl.BlockSpec((None, None, d, tf), wu_map),
            pl.BlockSpec((None, None, d_ff, td), wd_map),
            pl.BlockSpec((None, None, 1, tf), wg_map),
            pl.BlockSpec((None, None, 1, tf), wu_map),
            pl.BlockSpec((None, None, 1, td), wd_map),
        ],
        out_specs=pl.BlockSpec((tm, td), o_map),
        scratch_shapes=[
            pltpu.VMEM((2, cfg.moe_gather, d), F32),
            pltpu.VMEM((tm, d), BF16),
            pltpu.VMEM((n_f, tm, tf), BF16),
            pltpu.VMEM((max(d, d_ff), max(2 * tf, td)), BF16),
            pltpu.SemaphoreType.DMA((2,)),
        ],
    )
    return pl.pallas_call(
        functools.partial(_moe_body, cfg=cfg, n_f=n_f),
        out_shape=jax.ShapeDtypeStruct((ni * tm, d), F32),
        grid_spec=grid_spec,
        compiler_params=_cparams(("arbitrary", "arbitrary")),
        name="moe_experts",
    )(item_e, item_nsub, n_items, tok_tbl, h2, w_gate_up, w_gate_up, w_down,
      b_gate_up.reshape(depth, e, 1, 2 * d_ff), b_gate_up.reshape(depth, e, 1, 2 * d_ff),
      b_down.reshape(depth, e, 1, d))


def _moe_plan(cfg, idx, rank, counts):
    t, k = idx.shape
    tm, sb = cfg.moe_rows, cfg.moe_sub
    e = cfg.n_experts
    ni = e + (t * k) // tm
    items_per = (counts + tm - 1) // tm
    item_end = jnp.cumsum(items_per)
    item_start = item_end - items_per
    n_items = item_end[-1]
    ids = jnp.arange(ni, dtype=I32)
    item_e = jnp.minimum(jnp.searchsorted(item_end, ids, side="right"), e - 1).astype(I32)
    within = ids - item_start[item_e]
    rows = jnp.clip(counts[item_e] - within * tm, 0, tm)
    rows = jnp.where(ids < n_items, rows, 0)
    item_nsub = ((rows + sb - 1) // sb).astype(I32)
    dest = (item_start[idx] + rank // tm) * tm + rank % tm
    tok = jnp.broadcast_to(jnp.arange(t, dtype=I32)[:, None], (t, k))
    tok_tbl = jnp.zeros((ni * tm,), I32).at[dest.reshape(-1)].set(tok.reshape(-1), unique_indices=True)
    return tok_tbl.reshape(ni, 1, tm), item_e, item_nsub, n_items.reshape(1).astype(I32), dest.astype(I32)


def _combine_body(dest_ref, ys_hbm, x_ref, gate_ref, g2_ref, o_ref, buf_ref, sem, *, top_k, rows):
    def one(r, carry):
        for k in range(top_k):
            slot = dest_ref[0, r * top_k + k]
            pltpu.make_async_copy(ys_hbm.at[pl.ds(slot, 1), :], buf_ref.at[k, pl.ds(r, 1), :], sem.at[0]).start()
        return carry

    lax.fori_loop(0, rows, one, 0)
    for k in range(top_k):
        pltpu.make_async_copy(ys_hbm.at[pl.ds(0, rows), :], buf_ref.at[k], sem.at[0]).wait()
    gates = gate_ref[...]
    acc = gates[:, 0:1] * buf_ref[0]
    for k in range(1, top_k):
        acc = acc + gates[:, k:k + 1] * buf_ref[k]
    o_ref[...] = x_ref[...] + g2_ref[...] * acc


def _combine_call(cfg, x, ys, gates, dest, mod_rows, layer, gate_chunk):
    t, d = x.shape
    tc = cfg.dec_batch
    k = cfg.top_k
    bidx = functools.partial(_mod_block_index, cfg, tm=tc)
    return pl.pallas_call(
        functools.partial(_combine_body, top_k=k, rows=tc),
        out_shape=jax.ShapeDtypeStruct((t, d), F32),
        grid=(t // tc,),
        in_specs=[
            pl.BlockSpec((None, 1, tc * k), lambda i: (i, 0, 0), memory_space=pltpu.SMEM),
            pl.BlockSpec(memory_space=pl.ANY),
            pl.BlockSpec((tc, d), lambda i: (i, 0)),
            pl.BlockSpec((tc, LANES), lambda i: (i, 0)),
            pl.BlockSpec((None, tc, d), lambda i: (layer, bidx(i), gate_chunk)),
        ],
        out_specs=pl.BlockSpec((tc, d), lambda i: (i, 0)),
        scratch_shapes=[pltpu.VMEM((k, tc, d), F32), pltpu.SemaphoreType.DMA((1,))],
        compiler_params=_cparams(("arbitrary",)),
        name="moe_combine",
    )(dest.reshape(t // tc, 1, tc * k), ys, x, gates, mod_rows)


def _final_norm_body(x_ref, g_ref, o_ref):
    x = x_ref[...]
    o_ref[...] = x * lax.rsqrt(jnp.mean(x * x, axis=-1, keepdims=True) + EPS) * g_ref[...]


def _final_norm_call(cfg, x, g):
    t, d = x.shape
    tm = _pick(t, 512, SUBLANES)
    return pl.pallas_call(
        _final_norm_body,
        out_shape=jax.ShapeDtypeStruct((t, d), F32),
        grid=(t // tm,),
        in_specs=[pl.BlockSpec((tm, d), lambda i: (i, 0)), pl.BlockSpec((1, d), lambda i: (0, 0))],
        out_specs=pl.BlockSpec((tm, d), lambda i: (i, 0)),
        compiler_params=_cparams(("arbitrary",)),
        name="final_norm",
    )(x, g.reshape(1, d))


def _forward(cfg, x_prompt, x_sample, c_prompt, c_sample, state_s5_re, state_s5_im, state_ssd, state_conv, w):
    d = cfg.d_model
    nb, nseq = cfg.dec_batch, cfg.dec_seq
    assert cfg.seq % cfg.ts == 0 and cfg.seq % cfg.ssd_chunk == 0 and cfg.tp % cfg.ts == 0
    x = jnp.concatenate([x_prompt.reshape(cfg.tp, d),
                         jnp.swapaxes(x_sample, 0, 1).reshape(cfg.ts, d)], axis=0)
    c_all = jnp.concatenate([c_prompt, c_sample], axis=0)
    rc = c_all.shape[0]
    c_all = jnp.pad(c_all, ((0, (-rc) % 16), (0, 0)))
    mod = _ada_call(c_all, w["w_ada"], w["b_ada"])
    mod_rows = jnp.concatenate([jnp.repeat(mod[:, :cfg.batch], nb, axis=1), mod[:, cfg.batch:cfg.batch + nb]],
                               axis=1)
    o2 = cfg.d_ssd + cfg.conv_dim
    o3 = o2 + cfg.heads
    p_re, p_im, p_h, p_conv, s_re, s_im, s_h, s_conv = ([] for _ in range(8))
    for l in range(cfg.depth):
        w_in = w["w_in"][l]
        w_in_r = jnp.concatenate([w_in[:, :o2], w_in[:, o3:], w_in[:, o2:o3],
                                  jnp.zeros((d, LANES - cfg.heads), F32)], axis=1)
        h = _norm_mod_call(cfg, x, w["norm1_g"][l], mod_rows, l, 0, 1)
        proj = _mm_call(cfg, h, w_in_r)
        ssd_params = (w["conv_w"][l], w["conv_b"][l], w["dt_bias"][l], w["a_log"][l], w["ssd_d"][l],
                      w["ssd_norm_g"][l])
        y_ssd_p, h_p, conv_p = _ssd_call(cfg, proj, ssd_params)
        proj_s = jnp.swapaxes(proj[cfg.tp:].reshape(nseq, nb, cfg.proj_w), 0, 1)
        proj_s = jnp.pad(proj_s, ((0, 0), (0, SUBLANES - nseq), (0, 0)))
        y_ssd_s, h_s, conv_s = _ssd_call(cfg, proj_s, ssd_params, init=(state_ssd[l], state_conv[l]))
        y_ssd = jnp.concatenate([y_ssd_p, jnp.swapaxes(y_ssd_s[:, :nseq], 0, 1).reshape(cfg.ts, cfg.d_ssd)],
                                axis=0)
        tables = _s5_tables(cfg, w["s5_lambda_re"][l], w["s5_lambda_im"][l], w["s5_log_step"][l],
                            w["s5_b_re"][l], w["s5_b_im"][l], w["s5_c_re"][l], w["s5_c_im"][l])
        d_s5 = w["s5_d"][l].reshape(-1)
        yg_p, re_p, im_p = _s5_prompt_call(cfg, proj, tables, d_s5)
        yg_s, re_s, im_s = _s5_sample_call(cfg, proj, tables, d_s5, state_s5_re[l], state_s5_im[l])
        y_s5 = _glu_norm_call(cfg, jnp.concatenate([yg_p, yg_s], axis=0), w["s5_w_glu"][l], w["s5_norm_g"][l])
        x = _outproj_call(cfg, y_ssd, y_s5, w["w_out"][l], x, mod_rows, l, 2)
        h2, meta_i, meta_f, counts = _router_call(cfg, x, w["norm2_g"][l], mod_rows, l, 3, 4,
                                                  w["w_router"][l], w["b_router"][l])
        k = cfg.top_k
        plan = _moe_plan(cfg, meta_i[:, :k], meta_i[:, k:2 * k], counts[0, :cfg.n_experts].astype(I32))
        tok_tbl, item_e, item_nsub, n_items, dest = plan
        ys = _moe_call(cfg, l, h2, tok_tbl, item_e, item_nsub, n_items,
                       w["w_gate_up"], w["b_gate_up"], w["w_down"], w["b_down"])
        x = _combine_call(cfg, x, ys, meta_f, dest, mod_rows, l, 5)
        gshape = (cfg.s5_groups, cfg.s5_state)
        p_re.append(re_p.reshape((cfg.batch,) + gshape))
        p_im.append(im_p.reshape((cfg.batch,) + gshape))
        p_h.append(h_p)
        p_conv.append(conv_p)
        s_re.append(re_s.reshape((nb,) + gshape))
        s_im.append(im_s.reshape((nb,) + gshape))
        s_h.append(h_s)
        s_conv.append(conv_s)
    y = _final_norm_call(cfg, x, w["final_norm_g"])
    y_prompt = y[:cfg.tp].reshape(cfg.batch, cfg.seq, d)
    y_sample = jnp.swapaxes(y[cfg.tp:].reshape(nseq, nb, d), 0, 1)
    st = jnp.stack
    return (y_prompt, y_sample, st(p_re), st(p_im), st(p_h), st(p_conv), st(s_re), st(s_im), st(s_h), st(s_conv))


def kernel(x_prompt, x_sample, c_prompt, c_sample, state_s5_re, state_s5_im, state_ssd, state_conv,
           norm1_g, norm2_g, w_ada, b_ada, w_in, conv_w, conv_b, dt_bias, a_log, ssd_d, ssd_norm_g,
           s5_lambda_re, s5_lambda_im, s5_log_step, s5_b_re, s5_b_im, s5_c_re, s5_c_im, s5_d,
           s5_w_glu, s5_norm_g, w_out, w_router, b_router, w_gate_up, b_gate_up, w_down, b_down,
           final_norm_g):
    w = dict(norm1_g=norm1_g, norm2_g=norm2_g, w_ada=w_ada, b_ada=b_ada, w_in=w_in, conv_w=conv_w,
             conv_b=conv_b, dt_bias=dt_bias, a_log=a_log, ssd_d=ssd_d, ssd_norm_g=ssd_norm_g,
             s5_lambda_re=s5_lambda_re, s5_lambda_im=s5_lambda_im, s5_log_step=s5_log_step,
             s5_b_re=s5_b_re, s5_b_im=s5_b_im, s5_c_re=s5_c_re, s5_c_im=s5_c_im, s5_d=s5_d,
             s5_w_glu=s5_w_glu, s5_norm_g=s5_norm_g, w_out=w_out, w_router=w_router, b_router=b_router,
             w_gate_up=w_gate_up, b_gate_up=b_gate_up, w_down=w_down, b_down=b_down,
             final_norm_g=final_norm_g)
    return _forward(Cfg(), x_prompt, x_sample, c_prompt, c_sample, state_s5_re, state_s5_im, state_ssd,
                    state_conv, w)
```

```python
import functools
from typing import NamedTuple

import jax
import jax.numpy as jnp
from jax import lax
from jax.experimental import pallas as pl
from jax.experimental.pallas import tpu as pltpu

F32 = jnp.float32
BF16 = jnp.bfloat16
I32 = jnp.int32

LANES = 128
SUBLANES = 8
VMEM_LIMIT_MB = 60
EPS = 1e-5
N_MOD = 6
SWIGLU_LIMIT = 7.0
SWIGLU_ALPHA = 1.702
NEG_BIG = -1e30


class Cfg(NamedTuple):
    d_model: int = 4096
    batch: int = 4
    seq: int = 2048
    depth: int = 2
    dec_batch: int = 128
    dec_seq: int = 4
    d_ssd: int = 2048
    d_s5: int = 2048
    headdim: int = 64
    ssd_groups: int = 8
    ssd_state: int = 128
    conv_w: int = 4
    s5_group: int = 16
    s5_state: int = 64
    n_experts: int = 32
    top_k: int = 4
    d_ff: int = 4096
    ssd_chunk: int = 128
    s5_rows: int = 256
    moe_rows: int = 1280
    moe_unit: int = 128
    moe_gather: int = 64
    moe_tf: int = 256
    moe_td: int = 256
    mm_tm: int = 1088
    mm_tn: int = 640

    @property
    def heads(self):
        return self.d_ssd // self.headdim

    @property
    def gn(self):
        return self.ssd_groups * self.ssd_state

    @property
    def conv_dim(self):
        return self.d_ssd + 2 * self.gn

    @property
    def s5_groups(self):
        return self.d_s5 // self.s5_group

    @property
    def tp(self):
        return self.batch * self.seq

    @property
    def ts(self):
        return self.dec_batch * self.dec_seq

    @property
    def t(self):
        return self.tp + self.ts

    @property
    def proj_w(self):
        return 2 * self.d_ssd + 2 * self.gn + self.d_s5 + LANES


def _pick(n, cap, mult):
    best = None
    for d in range(mult, min(n, cap) + 1, mult):
        if n % d == 0:
            best = d
    assert best is not None, (n, cap, mult)
    return best


def _cparams(sem, **kw):
    return pltpu.CompilerParams(dimension_semantics=sem, vmem_limit_bytes=VMEM_LIMIT_MB * 1024 * 1024, **kw)


def _sigmoid(x):
    return 1.0 / (1.0 + jnp.exp(-x))


def _split3(x):
    h = x.astype(BF16)
    r = x - h.astype(F32)
    m = r.astype(BF16)
    lo = (r - m.astype(F32)).astype(BF16)
    return h, m, lo


def _dot(a, b):
    return jnp.dot(a, b, preferred_element_type=F32)


def _mod_block_index(cfg, i, tm):
    return jnp.where(i < cfg.tp // tm, (i * tm) // cfg.seq, cfg.batch)


def _ada_body(c_ref, w_ref, b_ref, o_ref, a_ref):
    @pl.when(jnp.logical_and(pl.program_id(0) == 0, pl.program_id(1) == 0))
    def _():
        c = c_ref[...]
        a_ref[...] = (c * _sigmoid(c)).astype(BF16)

    o_ref[...] = _dot(a_ref[...], w_ref[...].astype(BF16)) + b_ref[...]


def _ada_call(c_all, w_ada, b_ada):
    depth, d, n6 = w_ada.shape
    rc = c_all.shape[0]
    tn = _pick(n6, 1024, LANES)
    return pl.pallas_call(
        _ada_body,
        out_shape=jax.ShapeDtypeStruct((depth, rc, n6), F32),
        grid=(depth, n6 // tn),
        in_specs=[
            pl.BlockSpec((rc, d), lambda l, n: (0, 0)),
            pl.BlockSpec((None, d, tn), lambda l, n: (l, 0, n)),
            pl.BlockSpec((None, 1, tn), lambda l, n: (l, 0, n)),
        ],
        out_specs=pl.BlockSpec((None, rc, tn), lambda l, n: (l, 0, n)),
        scratch_shapes=[pltpu.VMEM((rc, d), BF16)],
        compiler_params=_cparams(("arbitrary", "arbitrary")),
        name="ada",
    )(c_all, w_ada, b_ada.reshape(depth, 1, n6))


def _rms_mod(x, g, sc, sh, nrep, nb):
    d = x.shape[-1]
    y = x * lax.rsqrt(jnp.mean(x * x, axis=-1, keepdims=True) + EPS) * g
    y3 = y.reshape(nrep, nb, d)
    h = y3 * (1.0 + sc[None]) + sh[None]
    return h.reshape(nrep * nb, d)


def _norm_mod_body(x_ref, g_ref, sh_ref, sc_ref, o_ref, *, nrep, nb):
    h = _rms_mod(x_ref[...], g_ref[...], sc_ref[...], sh_ref[...], nrep, nb)
    o_ref[...] = h.astype(o_ref.dtype)


def _norm_mod_call(cfg, x, g, mod_rows, layer, shift_chunk, scale_chunk):
    t, d = x.shape
    nb = cfg.dec_batch
    tm = cfg.ts
    bidx = functools.partial(_mod_block_index, cfg, tm=tm)
    return pl.pallas_call(
        functools.partial(_norm_mod_body, nrep=tm // nb, nb=nb),
        out_shape=jax.ShapeDtypeStruct((t, d), BF16),
        grid=(t // tm,),
        in_specs=[
            pl.BlockSpec((tm, d), lambda i: (i, 0)),
            pl.BlockSpec((1, d), lambda i: (0, 0)),
            pl.BlockSpec((None, nb, d), lambda i: (layer, bidx(i), shift_chunk)),
            pl.BlockSpec((None, nb, d), lambda i: (layer, bidx(i), scale_chunk)),
        ],
        out_specs=pl.BlockSpec((tm, d), lambda i: (i, 0)),
        compiler_params=_cparams(("arbitrary",)),
        name="norm_mod",
    )(x, g.reshape(1, d), mod_rows, mod_rows)


def _mm_body(a_ref, w_ref, o_ref, wbf_ref):
    @pl.when(pl.program_id(1) == 0)
    def _():
        wbf_ref[...] = w_ref[...].astype(BF16)

    o_ref[...] = _dot(a_ref[...], wbf_ref[...]).astype(o_ref.dtype)


def _mm_call(cfg, a, w):
    m, k = a.shape
    n = w.shape[1]
    tm = _pick(m, cfg.mm_tm, 16)
    tn = _pick(n, cfg.mm_tn, LANES)
    return pl.pallas_call(
        _mm_body,
        out_shape=jax.ShapeDtypeStruct((m, n), F32),
        grid=(n // tn, m // tm),
        in_specs=[
            pl.BlockSpec((tm, k), lambda j, i: (i, 0)),
            pl.BlockSpec((k, tn), lambda j, i: (0, j)),
        ],
        out_specs=pl.BlockSpec((tm, tn), lambda j, i: (i, j)),
        scratch_shapes=[pltpu.VMEM((k, tn), BF16)],
        compiler_params=_cparams(("arbitrary", "arbitrary")),
        name="in_proj",
    )(a, w)


def _ssd_body(*refs, cfg, q_in, l_real, has_init):
    if has_init:
        (dsk_ref, z_ref, xs_ref, bc_ref, dt_ref, cw_ref, cb_ref, dtb_ref, alog_ref, ng_ref,
         h0_ref, tail0_ref, y_ref, hout_ref, cout_ref, tail_ref, ext_ref, yt_ref) = refs
    else:
        (dsk_ref, z_ref, xs_ref, bc_ref, dt_ref, cw_ref, cb_ref, dtb_ref, alog_ref, ng_ref,
         y_ref, hout_ref, cout_ref, tail_ref, ext_ref, yt_ref) = refs
    q = cfg.ssd_chunk
    p = cfg.headdim
    n = cfg.ssd_state
    g_cnt = cfg.ssd_groups
    heads = cfg.heads
    hpg = heads // g_cnt
    d_ssd = cfg.d_ssd
    cdim = cfg.conv_dim
    c = pl.program_id(1)

    @pl.when(c == 0)
    def _():
        if has_init:
            hout_ref[...] = h0_ref[...]
            tail_ref[...] = jnp.concatenate(
                [jnp.zeros((SUBLANES - cfg.conv_w + 1, cdim), F32), tail0_ref[...]], axis=0)
        else:
            hout_ref[...] = jnp.zeros(hout_ref.shape, F32)
            tail_ref[...] = jnp.zeros(tail_ref.shape, F32)

    raw = jnp.concatenate([xs_ref[...], bc_ref[...]], axis=1)
    if q_in < q:
        raw = jnp.concatenate([raw, jnp.zeros((q - q_in, cdim), F32)], axis=0)
    ext_ref[0:SUBLANES, :] = tail_ref[...]
    ext_ref[SUBLANES:SUBLANES + q, :] = raw
    conv = jnp.broadcast_to(cb_ref[...], (q, cdim))
    for k in range(cfg.conv_w):
        conv = conv + ext_ref[pl.ds(SUBLANES - cfg.conv_w + 1 + k, q), :] * cw_ref[k:k + 1, :]
    tail_ref[...] = ext_ref[pl.ds(q, SUBLANES), :]
    cout_ref[...] = ext_ref[pl.ds(SUBLANES + l_real - (cfg.conv_w - 1), cfg.conv_w - 1), :]

    act = conv * _sigmoid(conv)
    xs_a = act[:, :d_ssd]
    bm = act[:, d_ssd:d_ssd + cfg.gn]
    cm = act[:, d_ssd + cfg.gn:]

    dtv = dt_ref[...] + dtb_ref[...]
    if q_in < q:
        dtv = jnp.concatenate([dtv, jnp.zeros((q - q_in, LANES), F32)], axis=0)
    dtv = jnp.maximum(dtv, 0.0) + jnp.log(1.0 + jnp.exp(-jnp.abs(dtv)))
    if l_real < q:
        rows = lax.broadcasted_iota(I32, (q, LANES), 0)
        dtv = jnp.where(rows < l_real, dtv, 0.0)
    da = dtv * (-jnp.exp(alog_ref[...]))
    ri = lax.broadcasted_iota(I32, (q, q), 0)
    ci = lax.broadcasted_iota(I32, (q, q), 1)
    tril = (ri >= ci).astype(BF16)
    d_h, d_m, d_l = _split3(da)
    cs = _dot(tril, d_h) + _dot(tril, d_m) + _dot(tril, d_l)
    cs_t = cs.T
    dt_t = dtv.T
    xs_t = xs_a.T
    causal_t = ci >= ri

    for g in range(g_cnt):
        bm_g = bm[:, g * n:(g + 1) * n].astype(BF16)
        cm_t = cm[:, g * n:(g + 1) * n].T.astype(BF16)
        cb_t = _dot(bm_g, cm_t)
        for hh in range(hpg):
            h = g * hpg + hh
            cs_col = cs[:, h:h + 1]
            cs_row = cs_t[h:h + 1, :]
            l_t = cb_t * jnp.exp(jnp.where(causal_t, cs_row - cs_col, -jnp.inf))
            x_t = xs_t[h * p:(h + 1) * p, :]
            xdt_t = x_t * dt_t[h:h + 1, :]
            y_diag = _dot(xdt_t.astype(BF16), l_t.astype(BF16))
            hs = hout_ref[h]
            y_off = _dot(hs.astype(BF16), cm_t) * jnp.exp(cs_row)
            cs_last = cs_row[:, q - 1:q]
            xw_t = xdt_t * jnp.exp(cs_last - cs_row)
            hout_ref[h] = jnp.exp(cs_last) * hs + _dot(xw_t.astype(BF16), bm_g)
            yt_ref[h * p:(h + 1) * p, :] = y_diag + y_off + dsk_ref[h] * x_t

    y = yt_ref[...].T
    z = z_ref[...]
    if q_in < q:
        y = y[:q_in]
    yz = y * (z * _sigmoid(z))
    gw = d_ssd // g_cnt
    parts = []
    for g in range(g_cnt):
        seg = yz[:, g * gw:(g + 1) * gw]
        parts.append(seg * lax.rsqrt(jnp.mean(seg * seg, axis=-1, keepdims=True) + EPS))
    y_ref[...] = (jnp.concatenate(parts, axis=1) * ng_ref[...]).astype(y_ref.dtype)


def _ssd_call(cfg, proj, params, init=None, layer=0):
    q = cfg.ssd_chunk
    d_ssd, gn2, cdim = cfg.d_ssd, 2 * cfg.gn, cfg.conv_dim
    heads, p, n = cfg.heads, cfg.headdim, cfg.ssd_state
    assert (2 * d_ssd) % gn2 == 0 and heads <= LANES
    bc_blk = (2 * d_ssd) // gn2
    dt_blk = (2 * d_ssd + gn2 + cfg.d_s5) // LANES
    conv_w, conv_b, dt_bias, a_log, d_skip, norm_g = params
    pad = LANES - heads
    small = [
        conv_w, conv_b.reshape(1, cdim),
        jnp.pad(dt_bias, (0, pad)).reshape(1, LANES), jnp.pad(a_log, (0, pad)).reshape(1, LANES),
        norm_g.reshape(1, d_ssd),
    ]
    small_specs = [
        pl.BlockSpec((cfg.conv_w, cdim), lambda b, c: (0, 0)),
        pl.BlockSpec((1, cdim), lambda b, c: (0, 0)),
        pl.BlockSpec((1, LANES), lambda b, c: (0, 0)),
        pl.BlockSpec((1, LANES), lambda b, c: (0, 0)),
        pl.BlockSpec((1, d_ssd), lambda b, c: (0, 0)),
    ]
    smem = pl.BlockSpec(memory_space=pltpu.SMEM)
    if init is None:
        nb, nc, q_in, l_real = cfg.batch, cfg.seq // q, q, q
        row_specs = [
            pl.BlockSpec((q, d_ssd), lambda b, c: (b * nc + c, 0)),
            pl.BlockSpec((q, d_ssd), lambda b, c: (b * nc + c, 1)),
            pl.BlockSpec((q, gn2), lambda b, c: (b * nc + c, bc_blk)),
            pl.BlockSpec((q, LANES), lambda b, c: (b * nc + c, dt_blk)),
        ]
        init_args, init_specs = [], []
        y_shape = jax.ShapeDtypeStruct((cfg.tp, d_ssd), BF16)
        y_spec = pl.BlockSpec((q, d_ssd), lambda b, c: (b * nc + c, 0))
    else:
        nb, nc, q_in, l_real = cfg.dec_batch, 1, SUBLANES, cfg.dec_seq
        row_specs = [
            pl.BlockSpec((None, q_in, d_ssd), lambda b, c: (b, 0, 0)),
            pl.BlockSpec((None, q_in, d_ssd), lambda b, c: (b, 0, 1)),
            pl.BlockSpec((None, q_in, gn2), lambda b, c: (b, 0, bc_blk)),
            pl.BlockSpec((None, q_in, LANES), lambda b, c: (b, 0, dt_blk)),
        ]
        init_args = list(init)
        init_specs = [
            pl.BlockSpec((None, None, heads, p, n), lambda b, c: (layer, b, 0, 0, 0)),
            pl.BlockSpec((None, None, cfg.conv_w - 1, cdim), lambda b, c: (layer, b, 0, 0)),
        ]
        y_shape = jax.ShapeDtypeStruct((nb, q_in, d_ssd), BF16)
        y_spec = pl.BlockSpec((None, q_in, d_ssd), lambda b, c: (b, 0, 0))
    return pl.pallas_call(
        functools.partial(_ssd_body, cfg=cfg, q_in=q_in, l_real=l_real, has_init=init is not None),
        out_shape=[
            y_shape,
            jax.ShapeDtypeStruct((nb, heads, p, n), F32),
            jax.ShapeDtypeStruct((nb, cfg.conv_w - 1, cdim), F32),
        ],
        grid=(nb, nc),
        in_specs=[smem] + row_specs + small_specs + init_specs,
        out_specs=[
            y_spec,
            pl.BlockSpec((None, heads, p, n), lambda b, c: (b, 0, 0, 0)),
            pl.BlockSpec((None, cfg.conv_w - 1, cdim), lambda b, c: (b, 0, 0)),
        ],
        scratch_shapes=[
            pltpu.VMEM((SUBLANES, cdim), F32),
            pltpu.VMEM((q + 2 * SUBLANES, cdim), F32),
            pltpu.VMEM((d_ssd, q), F32),
        ],
        compiler_params=_cparams(("arbitrary", "arbitrary")),
        name="ssd_sample" if init is not None else "ssd_prompt",
    )(d_skip, proj, proj, proj, proj, *small, *init_args)


S5_TILE_GROUPS = LANES // 16


def _s5_tables(cfg, lam_re, lam_im, log_step, b_re, b_im, c_re, c_im):
    g_cnt, n, k = cfg.s5_groups, cfg.s5_state, cfg.s5_group
    tg = LANES // k
    n_lt = g_cnt // tg
    lam = lax.complex(lam_re, lam_im)
    delta = jnp.exp(log_step)[:, None]
    lam_bar = jnp.exp(lam * delta)
    b_bar = ((lam_bar - 1.0) / lam)[..., None] * lax.complex(b_re, b_im)
    eye = jnp.eye(tg, dtype=F32)

    def blockdiag(m):
        a, b = m.shape[1], m.shape[2]
        m4 = m.reshape(n_lt, tg, a, b)
        return (m4[:, :, :, None, :] * eye[None, :, None, :, None]).reshape(n_lt, tg * a, tg * b)

    bt = jnp.swapaxes(b_bar, 1, 2)
    bdb = jnp.concatenate([blockdiag(jnp.real(bt)), blockdiag(jnp.imag(bt))], axis=-1).astype(BF16)
    ct_re = jnp.swapaxes(c_re, 1, 2)
    ct_im = jnp.swapaxes(c_im, 1, 2)
    bdc = jnp.concatenate([blockdiag(ct_re), blockdiag(-ct_im)], axis=1).astype(BF16)
    pows = []
    cur = lam_bar
    for _ in range(SUBLANES):
        flat = cur.reshape(n_lt, tg * n)
        pows.append(jnp.concatenate([jnp.real(flat), jnp.imag(flat)], axis=-1))
        cur = cur * cur
    pw = jnp.stack(pows, axis=1)
    return bdb, bdc, pw


def _gelu(y):
    return 0.5 * y * (1.0 + jnp.tanh(0.7978845608028654 * (y + 0.044715 * y * y * y)))


def _s5_prompt_body(u_ref, bdb_ref, bdc_ref, pw_ref, d_ref, y_ref, sre_ref, sim_ref, car_ref, *, rows, half):
    t = pl.program_id(2)

    @pl.when(t == 0)
    def _():
        car_ref[...] = jnp.zeros(car_ref.shape, F32)

    u = u_ref[...]
    bu = _dot(u.astype(BF16), bdb_ref[...])
    xr = bu[:, :half]
    xi = bu[:, half:]
    ridx = lax.broadcasted_iota(I32, (rows, half), 0)
    lr = pw_ref[0:1, :half]
    li = pw_ref[0:1, half:]
    cr = car_ref[0:1, :half]
    cim = car_ref[0:1, half:]
    first = ridx == 0
    xr = xr + jnp.where(first, lr * cr - li * cim, 0.0)
    xi = xi + jnp.where(first, lr * cim + li * cr, 0.0)
    step = 1
    k = 0
    while step < rows:
        pr = pw_ref[k:k + 1, :half]
        pi = pw_ref[k:k + 1, half:]
        keep = ridx >= step
        sr = jnp.where(keep, pltpu.roll(xr, step, 0), 0.0)
        si = jnp.where(keep, pltpu.roll(xi, step, 0), 0.0)
        xr, xi = xr + pr * sr - pi * si, xi + pr * si + pi * sr
        step *= 2
        k += 1
    car_ref[0:1, :half] = xr[rows - 1:rows, :]
    car_ref[0:1, half:] = xi[rows - 1:rows, :]
    sre_ref[...] = xr[rows - 1:rows, :]
    sim_ref[...] = xi[rows - 1:rows, :]
    s_cat = jnp.concatenate([xr, xi], axis=1).astype(BF16)
    y = _dot(s_cat, bdc_ref[...]) + d_ref[...] * u
    y_ref[...] = _gelu(y)


def _s5_prompt_call(cfg, proj, tables, d_skip):
    bdb, bdc, pw = tables
    n_lt = bdb.shape[0]
    half = bdb.shape[2] // 2
    rows = cfg.s5_rows
    assert rows <= 2 ** SUBLANES and cfg.seq % rows == 0
    nt = cfg.seq // rows
    u_blk = (2 * cfg.d_ssd + 2 * cfg.gn) // LANES
    return pl.pallas_call(
        functools.partial(_s5_prompt_body, rows=rows, half=half),
        out_shape=[
            jax.ShapeDtypeStruct((cfg.tp, cfg.d_s5), F32),
            jax.ShapeDtypeStruct((cfg.batch, n_lt, 1, half), F32),
            jax.ShapeDtypeStruct((cfg.batch, n_lt, 1, half), F32),
        ],
        grid=(cfg.batch, n_lt, nt),
        in_specs=[
            pl.BlockSpec((rows, LANES), lambda b, lt, t: (b * nt + t, u_blk + lt)),
            pl.BlockSpec((None, LANES, 2 * half), lambda b, lt, t: (lt, 0, 0)),
            pl.BlockSpec((None, 2 * half, LANES), lambda b, lt, t: (lt, 0, 0)),
            pl.BlockSpec((None, SUBLANES, 2 * half), lambda b, lt, t: (lt, 0, 0)),
            pl.BlockSpec((1, LANES), lambda b, lt, t: (0, lt)),
        ],
        out_specs=[
            pl.BlockSpec((rows, LANES), lambda b, lt, t: (b * nt + t, lt)),
            pl.BlockSpec((None, None, 1, half), lambda b, lt, t: (b, lt, 0, 0)),
            pl.BlockSpec((None, None, 1, half), lambda b, lt, t: (b, lt, 0, 0)),
        ],
        scratch_shapes=[pltpu.VMEM((SUBLANES, 2 * half), F32)],
        compiler_params=_cparams(("arbitrary", "arbitrary", "arbitrary")),
        name="s5_prompt",
    )(proj, bdb, bdc, pw, d_skip.reshape(1, cfg.d_s5))


def _s5_sample_body(u_ref, bdb_ref, bdc_ref, pw_ref, d_ref, s0r_ref, s0i_ref, y_ref, sre_ref, sim_ref,
                    *, nb, nseq, half):
    u = u_ref[...]
    bu = _dot(u.astype(BF16), bdb_ref[...])
    lr = pw_ref[0:1, :half]
    li = pw_ref[0:1, half:]
    sr = s0r_ref[...]
    si = s0i_ref[...]
    st_r, st_i = [], []
    for s in range(nseq):
        br = bu[s * nb:(s + 1) * nb, :half]
        bi = bu[s * nb:(s + 1) * nb, half:]
        sr, si = lr * sr - li * si + br, lr * si + li * sr + bi
        st_r.append(sr)
        st_i.append(si)
    sre_ref[...] = sr
    sim_ref[...] = si
    s_cat = jnp.concatenate([jnp.concatenate(st_r, axis=0), jnp.concatenate(st_i, axis=0)], axis=1)
    y = _dot(s_cat.astype(BF16), bdc_ref[...]) + d_ref[...] * u
    y_ref[...] = _gelu(y)


def _s5_sample_call(cfg, proj, tables, d_skip, s_re0, s_im0, layer):
    bdb, bdc, pw = tables
    n_lt = bdb.shape[0]
    half = bdb.shape[2] // 2
    nb, nseq = cfg.dec_batch, cfg.dec_seq
    assert cfg.tp % cfg.ts == 0
    u_blk = (2 * cfg.d_ssd + 2 * cfg.gn) // LANES
    row_blk = cfg.tp // cfg.ts
    gsz = cfg.s5_groups * cfg.s5_state
    return pl.pallas_call(
        functools.partial(_s5_sample_body, nb=nb, nseq=nseq, half=half),
        out_shape=[
            jax.ShapeDtypeStruct((cfg.ts, cfg.d_s5), F32),
            jax.ShapeDtypeStruct((nb, gsz), F32),
            jax.ShapeDtypeStruct((nb, gsz), F32),
        ],
        grid=(n_lt,),
        in_specs=[
            pl.BlockSpec((cfg.ts, LANES), lambda lt: (row_blk, u_blk + lt)),
            pl.BlockSpec((None, LANES, 2 * half), lambda lt: (lt, 0, 0)),
            pl.BlockSpec((None, 2 * half, LANES), lambda lt: (lt, 0, 0)),
            pl.BlockSpec((None, SUBLANES, 2 * half), lambda lt: (lt, 0, 0)),
            pl.BlockSpec((1, LANES), lambda lt: (0, lt)),
            pl.BlockSpec((None, nb, half), lambda lt: (layer, 0, lt)),
            pl.BlockSpec((None, nb, half), lambda lt: (layer, 0, lt)),
        ],
        out_specs=[
            pl.BlockSpec((cfg.ts, LANES), lambda lt: (0, lt)),
            pl.BlockSpec((nb, half), lambda lt: (0, lt)),
            pl.BlockSpec((nb, half), lambda lt: (0, lt)),
        ],
        compiler_params=_cparams(("arbitrary",)),
        name="s5_sample",
    )(proj, bdb, bdc, pw, d_skip.reshape(1, cfg.d_s5), s_re0.reshape(-1, nb, gsz), s_im0.reshape(-1, nb, gsz))


def _glu_norm_body(y_ref, w_ref, g_ref, o_ref, wbf_ref):
    @pl.when(pl.program_id(0) == 0)
    def _():
        wbf_ref[...] = w_ref[...].astype(BF16)

    y = y_ref[...]
    v = y * _sigmoid(_dot(y.astype(BF16), wbf_ref[...]))
    v = v * lax.rsqrt(jnp.mean(v * v, axis=-1, keepdims=True) + EPS) * g_ref[...]
    o_ref[...] = v.astype(o_ref.dtype)


def _glu_norm_call(cfg, y, w_glu, g, layer):
    t, d = y.shape
    tm = _pick(t, cfg.mm_tm, 16)
    return pl.pallas_call(
        _glu_norm_body,
        out_shape=jax.ShapeDtypeStruct((t, d), BF16),
        grid=(t // tm,),
        in_specs=[
            pl.BlockSpec((tm, d), lambda i: (i, 0)),
            pl.BlockSpec((None, d, d), lambda i: (layer, 0, 0)),
            pl.BlockSpec((1, d), lambda i: (0, 0)),
        ],
        out_specs=pl.BlockSpec((tm, d), lambda i: (i, 0)),
        scratch_shapes=[pltpu.VMEM((d, d), BF16)],
        compiler_params=_cparams(("arbitrary",)),
        name="s5_glu_norm",
    )(y, w_glu, g.reshape(1, d))


def _outproj_body(ya_ref, yb_ref, w_ref, x_ref, g1_ref, o_ref, wbf_ref, *, d_a, nrep, nb):
    @pl.when(pl.program_id(1) == 0)
    def _():
        wbf_ref[...] = w_ref[...].astype(BF16)

    mix = _dot(ya_ref[...], wbf_ref[:d_a, :]) + _dot(yb_ref[...], wbf_ref[d_a:, :])
    tn = mix.shape[1]
    gated = (mix.reshape(nrep, nb, tn) * g1_ref[...][None]).reshape(nrep * nb, tn)
    o_ref[...] = x_ref[...] + gated


def _outproj_call(cfg, ya, yb, w_out, x, mod_rows, layer, gate_chunk):
    t, d = x.shape
    d_a, d_b = ya.shape[1], yb.shape[1]
    nb = cfg.dec_batch
    tm = cfg.ts
    tn = _pick(d, 512, LANES)
    bidx = functools.partial(_mod_block_index, cfg, tm=tm)
    nblk = d // tn
    return pl.pallas_call(
        functools.partial(_outproj_body, d_a=d_a, nrep=tm // nb, nb=nb),
        out_shape=jax.ShapeDtypeStruct((t, d), F32),
        grid=(nblk, t // tm),
        in_specs=[
            pl.BlockSpec((tm, d_a), lambda j, i: (i, 0)),
            pl.BlockSpec((tm, d_b), lambda j, i: (i, 0)),
            pl.BlockSpec((None, d_a + d_b, tn), lambda j, i: (layer, 0, j)),
            pl.BlockSpec((tm, tn), lambda j, i: (i, j)),
            pl.BlockSpec((None, nb, tn), lambda j, i: (layer, bidx(i), gate_chunk * nblk + j)),
        ],
        out_specs=pl.BlockSpec((tm, tn), lambda j, i: (i, j)),
        scratch_shapes=[pltpu.VMEM((d_a + d_b, tn), BF16)],
        compiler_params=_cparams(("arbitrary", "arbitrary")),
        name="out_proj",
    )(ya, yb, w_out, x, mod_rows)


def _router_body(x_ref, g_ref, sh_ref, sc_ref, wr_ref, br_ref, h_ref, mi_ref, mf_ref, cnt_ref,
                 whi_ref, wlo_ref, car_ref, *, nrep, nb, top_k):
    i = pl.program_id(0)

    @pl.when(i == 0)
    def _():
        w = wr_ref[...]
        hi = w.astype(BF16)
        whi_ref[...] = hi
        wlo_ref[...] = (w - hi.astype(F32)).astype(BF16)
        car_ref[...] = jnp.zeros(car_ref.shape, F32)

    h = _rms_mod(x_ref[...], g_ref[...], sc_ref[...], sh_ref[...], nrep, nb)
    h_ref[...] = h
    tm = h.shape[0]
    hi = h.astype(BF16)
    lo = (h - hi.astype(F32)).astype(BF16)
    logits = _dot(hi, whi_ref[...]) + (_dot(hi, wlo_ref[...]) + _dot(lo, whi_ref[...])) + br_ref[...]
    lane_i = lax.broadcasted_iota(I32, (tm, LANES), 1)
    lane = lane_i.astype(F32)
    vals = logits
    idxs, tops = [], []
    for _ in range(top_k):
        m = jnp.max(vals, axis=-1, keepdims=True)
        idx = jnp.min(jnp.where(vals == m, lane, float(LANES)), axis=-1, keepdims=True)
        idxs.append(idx)
        tops.append(m)
        vals = jnp.where(lane == idx, -jnp.inf, vals)
    exps = [jnp.exp(v - tops[0]) for v in tops]
    denom = exps[0]
    for e in exps[1:]:
        denom = denom + e
    onehot = jnp.zeros((tm, LANES), F32)
    for idx in idxs:
        onehot = onehot + (lane == idx).astype(F32)
    ri = lax.broadcasted_iota(I32, (tm, tm), 0)
    ci = lax.broadcasted_iota(I32, (tm, tm), 1)
    before = _dot((ri > ci).astype(BF16), onehot.astype(BF16)) + car_ref[0:1, :]
    car_ref[0:1, :] = car_ref[0:1, :] + jnp.sum(onehot, axis=0, keepdims=True)
    cnt_ref[...] = car_ref[0:1, :]
    mi = jnp.zeros((tm, LANES), I32)
    mf = jnp.zeros((tm, LANES), F32)
    for k in range(top_k):
        rank = jnp.sum(jnp.where(lane == idxs[k], before, 0.0), axis=-1, keepdims=True).astype(I32)
        mi = jnp.where(lane_i == k, idxs[k].astype(I32), mi)
        mi = jnp.where(lane_i == top_k + k, rank, mi)
        mf = jnp.where(lane_i == k, exps[k] / denom, mf)
    mi_ref[...] = mi
    mf_ref[...] = mf


def _router_call(cfg, x, g, mod_rows, layer, shift_chunk, scale_chunk, w_router, b_router):
    t, d = x.shape
    nb = cfg.dec_batch
    tm = cfg.ts
    e = cfg.n_experts
    assert e <= LANES and 2 * cfg.top_k <= LANES
    bidx = functools.partial(_mod_block_index, cfg, tm=tm)
    wr = jnp.pad(w_router, ((0, 0), (0, LANES - e)))
    br = jnp.pad(b_router, (0, LANES - e), constant_values=NEG_BIG).reshape(1, LANES)
    return pl.pallas_call(
        functools.partial(_router_body, nrep=tm // nb, nb=nb, top_k=cfg.top_k),
        out_shape=[
            jax.ShapeDtypeStruct((t, d), F32),
            jax.ShapeDtypeStruct((t, LANES), I32),
            jax.ShapeDtypeStruct((t, LANES), F32),
            jax.ShapeDtypeStruct((1, LANES), F32),
        ],
        grid=(t // tm,),
        in_specs=[
            pl.BlockSpec((tm, d), lambda i: (i, 0)),
            pl.BlockSpec((1, d), lambda i: (0, 0)),
            pl.BlockSpec((None, nb, d), lambda i: (layer, bidx(i), shift_chunk)),
            pl.BlockSpec((None, nb, d), lambda i: (layer, bidx(i), scale_chunk)),
            pl.BlockSpec((d, LANES), lambda i: (0, 0)),
            pl.BlockSpec((1, LANES), lambda i: (0, 0)),
        ],
        out_specs=[
            pl.BlockSpec((tm, d), lambda i: (i, 0)),
            pl.BlockSpec((tm, LANES), lambda i: (i, 0)),
            pl.BlockSpec((tm, LANES), lambda i: (i, 0)),
            pl.BlockSpec((1, LANES), lambda i: (0, 0)),
        ],
        scratch_shapes=[
            pltpu.VMEM((d, LANES), BF16),
            pltpu.VMEM((d, LANES), BF16),
            pltpu.VMEM((SUBLANES, LANES), F32),
        ],
        compiler_params=_cparams(("arbitrary",)),
        name="router",
    )(x, g.reshape(1, d), mod_rows, mod_rows, wr, br)


def _moe_body(ie_ref, nu_ref, nit_ref, tok_ref, h_hbm, wg_ref, wu_ref, wd_ref, bg_ref, bu_ref, bd_ref,
              o_ref, stage_ref, xb_ref, act_ref, sem, *, cfg, n_f):
    i = pl.program_id(0)
    j = pl.program_id(1)
    unit, gb = cfg.moe_unit, cfg.moe_gather
    td = cfg.moe_td

    @pl.when(i < nit_ref[0])
    def _():
        nu = nu_ref[i]
        n_big = lax.shift_right_logical(nu, 2)
        rem = nu - 4 * n_big
        base = n_big * (4 * unit)

        def for_blocks(fn):
            def big(s, carry):
                fn(pl.multiple_of(s * (4 * unit), 4 * unit), 4 * unit)
                return carry
            lax.fori_loop(0, n_big, big, 0)

            @pl.when(rem >= 2)
            def _():
                fn(pl.multiple_of(base, 2 * unit), 2 * unit)

            @pl.when(jnp.bitwise_and(rem, 1) == 1)
            def _():
                fn(pl.multiple_of(base + lax.shift_right_logical(rem, 1) * (2 * unit), unit), unit)

        @pl.when(jnp.logical_and(j == 0, nu > 0))
        def _():
            n_blk = nu * (unit // gb)

            def issue(blk, slot):
                def one(r, carry):
                    tok = tok_ref[0, blk * gb + r]
                    pltpu.make_async_copy(h_hbm.at[pl.ds(tok, 1), :], stage_ref.at[slot, pl.ds(r, 1), :],
                                          sem.at[slot]).start()
                    return carry
                lax.fori_loop(0, gb, one, 0, unroll=8)

            issue(0, 0)

            def blk_step(blk, carry):
                slot = blk % 2

                @pl.when(blk + 1 < n_blk)
                def _():
                    issue(blk + 1, 1 - slot)

                pltpu.make_async_copy(h_hbm.at[pl.ds(0, gb), :], stage_ref.at[slot], sem.at[slot]).wait()
                xb_ref[pl.ds(pl.multiple_of(blk * gb, gb), gb), :] = stage_ref[slot].astype(BF16)
                return carry

            lax.fori_loop(0, n_blk, blk_step, 0)

        @pl.when(j < n_f)
        def _():
            def blk(r0, size):
                x = xb_ref[pl.ds(r0, size), :]
                g = _dot(x, wg_ref[...].astype(BF16)) + bg_ref[...]
                u = _dot(x, wu_ref[...].astype(BF16)) + bu_ref[...]
                gl = jnp.minimum(g, SWIGLU_LIMIT)
                ul = jnp.clip(u, -SWIGLU_LIMIT, SWIGLU_LIMIT)
                a = gl * _sigmoid(SWIGLU_ALPHA * gl) * (ul + 1.0)
                act_ref[j, pl.ds(r0, size), :] = a.astype(BF16)

            for_blocks(blk)

        @pl.when(j >= n_f)
        def _():
            def blk(r0, size):
                a = jnp.concatenate([act_ref[jj, pl.ds(r0, size), :] for jj in range(n_f)], axis=1)
                o_ref[pl.ds(r0, size), :] = _dot(a, wd_ref[...].astype(BF16)) + bd_ref[...]

            for_blocks(blk)

            def clear(s, carry):
                o_ref[pl.ds(pl.multiple_of(s * unit, unit), unit), :] = jnp.zeros((unit, td), F32)
                return carry

            lax.fori_loop(nu, cfg.moe_rows // unit, clear, 0)

    @pl.when(jnp.logical_and(i >= nit_ref[0], j >= n_f))
    def _():
        o_ref[...] = jnp.zeros(o_ref.shape, F32)


def _moe_call(cfg, layer, h2, tok_tbl, item_e, item_nu, n_items, w_gate_up, b_gate_up, w_down, b_down):
    t, d = h2.shape
    tm, tf, td, d_ff = cfg.moe_rows, cfg.moe_tf, cfg.moe_td, cfg.d_ff
    ni = tok_tbl.shape[0]
    n_f, n_d = d_ff // tf, d // td
    depth, e = w_gate_up.shape[0], w_gate_up.shape[1]
    steps = n_f + n_d
    assert tm % cfg.moe_unit == 0 and cfg.moe_unit % cfg.moe_gather == 0

    def eff(i, j, nit):
        valid = i < nit[0]
        return jnp.where(valid, i, nit[0] - 1), jnp.where(valid, j, steps - 1)

    def wg_map(i, j, ie, ns, nit):
        i2, j2 = eff(i, j, nit)
        return (layer, ie[i2], 0, jnp.minimum(j2, n_f - 1))

    def wu_map(i, j, ie, ns, nit):
        i2, j2 = eff(i, j, nit)
        return (layer, ie[i2], 0, n_f + jnp.minimum(j2, n_f - 1))

    def wd_map(i, j, ie, ns, nit):
        i2, j2 = eff(i, j, nit)
        return (layer, ie[i2], 0, jnp.maximum(j2 - n_f, 0))

    def o_map(i, j, ie, ns, nit):
        return (i, jnp.maximum(j - n_f, 0))

    grid_spec = pltpu.PrefetchScalarGridSpec(
        num_scalar_prefetch=3,
        grid=(ni, steps),
        in_specs=[
            pl.BlockSpec((None, 1, tm), lambda i, j, ie, ns, nit: (jnp.minimum(i, nit[0] - 1), 0, 0),
                         memory_space=pltpu.SMEM),
            pl.BlockSpec(memory_space=pl.ANY),
            pl.BlockSpec((None, None, d, tf), wg_map),
            pl.BlockSpec((None, None, d, tf), wu_map),
            pl.BlockSpec((None, None, d_ff, td), wd_map),
            pl.BlockSpec((None, None, 1, tf), wg_map),
            pl.BlockSpec((None, None, 1, tf), wu_map),
            pl.BlockSpec((None, None, 1, td), wd_map),
        ],
        out_specs=pl.BlockSpec((tm, td), o_map),
        scratch_shapes=[
            pltpu.VMEM((2, cfg.moe_gather, d), F32),
            pltpu.VMEM((tm, d), BF16),
            pltpu.VMEM((n_f, tm, tf), BF16),
            pltpu.SemaphoreType.DMA((2,)),
        ],
    )
    return pl.pallas_call(
        functools.partial(_moe_body, cfg=cfg, n_f=n_f),
        out_shape=jax.ShapeDtypeStruct((ni * tm, d), F32),
        grid_spec=grid_spec,
        compiler_params=_cparams(("arbitrary", "arbitrary"), disable_bounds_checks=True),
        name="moe_experts",
    )(item_e, item_nu, n_items, tok_tbl, h2, w_gate_up, w_gate_up, w_down,
      b_gate_up.reshape(depth, e, 1, 2 * d_ff), b_gate_up.reshape(depth, e, 1, 2 * d_ff),
      b_down.reshape(depth, e, 1, d))


def _moe_plan(cfg, idx, rank, counts):
    t, k = idx.shape
    tm, unit = cfg.moe_rows, cfg.moe_unit
    e = cfg.n_experts
    ni = e + (t * k) // tm
    items_per = (counts + tm - 1) // tm
    item_end = jnp.cumsum(items_per)
    item_start = item_end - items_per
    n_items = item_end[-1]
    per = ((counts + jnp.maximum(items_per, 1) - 1) // jnp.maximum(items_per, 1) + unit - 1) // unit * unit
    per = jnp.maximum(per, unit)
    ids = jnp.arange(ni, dtype=I32)
    item_e = jnp.minimum(jnp.searchsorted(item_end, ids, side="right"), e - 1).astype(I32)
    within = ids - item_start[item_e]
    rows = jnp.clip(counts[item_e] - within * per[item_e], 0, per[item_e])
    rows = jnp.where(ids < n_items, rows, 0)
    item_nu = ((rows + unit - 1) // unit).astype(I32)
    per_a = per[idx]
    dest = (item_start[idx] + rank // per_a) * tm + rank % per_a
    tok = jnp.broadcast_to(jnp.arange(t, dtype=I32)[:, None], (t, k))
    tok_tbl = jnp.zeros((ni * tm,), I32).at[dest.reshape(-1)].set(tok.reshape(-1), unique_indices=True)
    return tok_tbl.reshape(ni, 1, tm), item_e, item_nu, n_items.reshape(1).astype(I32), dest.astype(I32)


def _combine_body(dest_ref, ys_hbm, x_ref, gate_ref, g2_ref, o_ref, buf_ref, sem, *, top_k, rows):
    def one(r, carry):
        for k in range(top_k):
            slot = dest_ref[0, r * top_k + k]
            pltpu.make_async_copy(ys_hbm.at[pl.ds(slot, 1), :], buf_ref.at[k, pl.ds(r, 1), :], sem.at[0]).start()
        return carry

    lax.fori_loop(0, rows, one, 0, unroll=4)
    for k in range(top_k):
        pltpu.make_async_copy(ys_hbm.at[pl.ds(0, rows), :], buf_ref.at[k], sem.at[0]).wait()
    gates = gate_ref[...]
    acc = gates[:, 0:1] * buf_ref[0]
    for k in range(1, top_k):
        acc = acc + gates[:, k:k + 1] * buf_ref[k]
    o_ref[...] = x_ref[...] + g2_ref[...] * acc


def _combine_call(cfg, x, ys, gates, dest, mod_rows, layer, gate_chunk):
    t, d = x.shape
    tc = cfg.dec_batch
    k = cfg.top_k
    bidx = functools.partial(_mod_block_index, cfg, tm=tc)
    return pl.pallas_call(
        functools.partial(_combine_body, top_k=k, rows=tc),
        out_shape=jax.ShapeDtypeStruct((t, d), F32),
        grid=(t // tc,),
        in_specs=[
            pl.BlockSpec((None, 1, tc * k), lambda i: (i, 0, 0), memory_space=pltpu.SMEM),
            pl.BlockSpec(memory_space=pl.ANY),
            pl.BlockSpec((tc, d), lambda i: (i, 0)),
            pl.BlockSpec((tc, LANES), lambda i: (i, 0)),
            pl.BlockSpec((None, tc, d), lambda i: (layer, bidx(i), gate_chunk)),
        ],
        out_specs=pl.BlockSpec((tc, d), lambda i: (i, 0)),
        scratch_shapes=[pltpu.VMEM((k, tc, d), F32), pltpu.SemaphoreType.DMA((1,))],
        compiler_params=_cparams(("arbitrary",), disable_bounds_checks=True),
        name="moe_combine",
    )(dest.reshape(t // tc, 1, tc * k), ys, x, gates, mod_rows)


def _final_norm_body(x_ref, g_ref, op_ref, os_ref, *, n_prompt):
    i = pl.program_id(0)
    x = x_ref[...]
    y = x * lax.rsqrt(jnp.mean(x * x, axis=-1, keepdims=True) + EPS) * g_ref[...]

    @pl.when(i < n_prompt)
    def _():
        op_ref[...] = y

    @pl.when(i >= n_prompt)
    def _():
        os_ref[...] = y


def _final_norm_call(cfg, x, g):
    t, d = x.shape
    tm = cfg.ts
    n_prompt = cfg.tp // tm
    return pl.pallas_call(
        functools.partial(_final_norm_body, n_prompt=n_prompt),
        out_shape=[jax.ShapeDtypeStruct((cfg.tp, d), F32), jax.ShapeDtypeStruct((cfg.ts, d), F32)],
        grid=(t // tm,),
        in_specs=[pl.BlockSpec((tm, d), lambda i: (i, 0)), pl.BlockSpec((1, d), lambda i: (0, 0))],
        out_specs=[pl.BlockSpec((tm, d), lambda i: (jnp.minimum(i, n_prompt - 1), 0)),
                   pl.BlockSpec((tm, d), lambda i: (0, 0))],
        compiler_params=_cparams(("arbitrary",)),
        name="final_norm",
    )(x, g.reshape(1, d))


def _forward(cfg, x_prompt, x_sample, c_prompt, c_sample, state_s5_re, state_s5_im, state_ssd, state_conv, w):
    d = cfg.d_model
    nb, nseq = cfg.dec_batch, cfg.dec_seq
    assert cfg.seq % cfg.ts == 0 and cfg.seq % cfg.ssd_chunk == 0 and cfg.tp % cfg.ts == 0
    x = jnp.concatenate([x_prompt.reshape(cfg.tp, d),
                         jnp.swapaxes(x_sample, 0, 1).reshape(cfg.ts, d)], axis=0)
    c_all = jnp.concatenate([c_prompt, c_sample], axis=0)
    rc = c_all.shape[0]
    c_all = jnp.pad(c_all, ((0, (-rc) % 16), (0, 0)))
    mod = _ada_call(c_all, w["w_ada"], w["b_ada"])
    mod_rows = jnp.concatenate([jnp.repeat(mod[:, :cfg.batch], nb, axis=1), mod[:, cfg.batch:cfg.batch + nb]],
                               axis=1)
    o2 = cfg.d_ssd + cfg.conv_dim
    o3 = o2 + cfg.heads
    p_re, p_im, p_h, p_conv, s_re, s_im, s_h, s_conv = ([] for _ in range(8))
    for l in range(cfg.depth):
        w_in = w["w_in"][l]
        w_in_r = jnp.concatenate([w_in[:, :o2], w_in[:, o3:], w_in[:, o2:o3],
                                  jnp.zeros((d, LANES - cfg.heads), F32)], axis=1)
        h = _norm_mod_call(cfg, x, w["norm1_g"][l], mod_rows, l, 0, 1)
        proj = _mm_call(cfg, h, w_in_r)
        ssd_params = (w["conv_w"][l], w["conv_b"][l], w["dt_bias"][l], w["a_log"][l], w["ssd_d"][l],
                      w["ssd_norm_g"][l])
        y_ssd_p, h_p, conv_p = _ssd_call(cfg, proj, ssd_params)
        proj_s = jnp.swapaxes(proj[cfg.tp:].reshape(nseq, nb, cfg.proj_w), 0, 1)
        proj_s = jnp.pad(proj_s, ((0, 0), (0, SUBLANES - nseq), (0, 0)))
        y_ssd_s, h_s, conv_s = _ssd_call(cfg, proj_s, ssd_params, init=(state_ssd, state_conv), layer=l)
        y_ssd = jnp.concatenate([y_ssd_p, jnp.swapaxes(y_ssd_s[:, :nseq], 0, 1).reshape(cfg.ts, cfg.d_ssd)],
                                axis=0)
        tables = _s5_tables(cfg, w["s5_lambda_re"][l], w["s5_lambda_im"][l], w["s5_log_step"][l],
                            w["s5_b_re"][l], w["s5_b_im"][l], w["s5_c_re"][l], w["s5_c_im"][l])
        d_s5 = w["s5_d"][l].reshape(-1)
        yg_p, re_p, im_p = _s5_prompt_call(cfg, proj, tables, d_s5)
        yg_s, re_s, im_s = _s5_sample_call(cfg, proj, tables, d_s5, state_s5_re, state_s5_im, l)
        y_s5 = _glu_norm_call(cfg, jnp.concatenate([yg_p, yg_s], axis=0), w["s5_w_glu"], w["s5_norm_g"][l], l)
        x = _outproj_call(cfg, y_ssd, y_s5, w["w_out"], x, mod_rows, l, 2)
        h2, meta_i, meta_f, counts = _router_call(cfg, x, w["norm2_g"][l], mod_rows, l, 3, 4,
                                                  w["w_router"][l], w["b_router"][l])
        k = cfg.top_k
        plan = _moe_plan(cfg, meta_i[:, :k], meta_i[:, k:2 * k], counts[0, :cfg.n_experts].astype(I32))
        tok_tbl, item_e, item_nu, n_items, dest = plan
        ys = _moe_call(cfg, l, h2, tok_tbl, item_e, item_nu, n_items,
                       w["w_gate_up"], w["b_gate_up"], w["w_down"], w["b_down"])
        x = _combine_call(cfg, x, ys, meta_f, dest, mod_rows, l, 5)
        gshape = (cfg.s5_groups, cfg.s5_state)
        p_re.append(re_p.reshape((cfg.batch,) + gshape))
        p_im.append(im_p.reshape((cfg.batch,) + gshape))
        p_h.append(h_p)
        p_conv.append(conv_p)
        s_re.append(re_s.reshape((nb,) + gshape))
        s_im.append(im_s.reshape((nb,) + gshape))
        s_h.append(h_s)
        s_conv.append(conv_s)
    y_p, y_s = _final_norm_call(cfg, x, w["final_norm_g"])
    y_prompt = y_p.reshape(cfg.batch, cfg.seq, d)
    y_sample = jnp.swapaxes(y_s.reshape(nseq, nb, d), 0, 1)
    st = jnp.stack
    return (y_prompt, y_sample, st(p_re), st(p_im), st(p_h), st(p_conv), st(s_re), st(s_im), st(s_h), st(s_conv))


def kernel(x_prompt, x_sample, c_prompt, c_sample, state_s5_re, state_s5_im, state_ssd, state_conv,
           norm1_g, norm2_g, w_ada, b_ada, w_in, conv_w, conv_b, dt_bias, a_log, ssd_d, ssd_norm_g,
           s5_lambda_re, s5_lambda_im, s5_log_step, s5_b_re, s5_b_im, s5_c_re, s5_c_im, s5_d,
           s5_w_glu, s5_norm_g, w_out, w_router, b_router, w_gate_up, b_gate_up, w_down, b_down,
           final_norm_g):
    w = dict(norm1_g=norm1_g, norm2_g=norm2_g, w_ada=w_ada, b_ada=b_ada, w_in=w_in, conv_w=conv_w,
             conv_b=conv_b, dt_bias=dt_bias, a_log=a_log, ssd_d=ssd_d, ssd_norm_g=ssd_norm_g,
             s5_lambda_re=s5_lambda_re, s5_lambda_im=s5_lambda_im, s5_log_step=s5_log_step,
             s5_b_re=s5_b_re, s5_b_im=s5_b_im, s5_c_re=s5_c_re, s5_c_im=s5_c_im, s5_d=s5_d,
             s5_w_glu=s5_w_glu, s5_norm_g=s5_norm_g, w_out=w_out, w_router=w_router, b_router=b_router,
             w_gate_up=w_gate_up, b_gate_up=b_gate_up, w_down=w_down, b_down=b_down,
             final_norm_g=final_norm_g)
    return _forward(Cfg(), x_prompt, x_sample, c_prompt, c_sample, state_s5_re, state_s5_im, state_ssd,
                    state_conv, w)
```

```python
import functools
from typing import NamedTuple

import jax
import jax.numpy as jnp
from jax import lax
from jax.experimental import pallas as pl
from jax.experimental.pallas import tpu as pltpu

F32 = jnp.float32
BF16 = jnp.bfloat16
I32 = jnp.int32

LANES = 128
SUBLANES = 8
VMEM_LIMIT_MB = 60
EPS = 1e-5
N_MOD = 6
SWIGLU_LIMIT = 7.0
SWIGLU_ALPHA = 1.702
NEG_BIG = -1e30


class Cfg(NamedTuple):
    d_model: int = 4096
    batch: int = 4
    seq: int = 2048
    depth: int = 2
    dec_batch: int = 128
    dec_seq: int = 4
    d_ssd: int = 2048
    d_s5: int = 2048
    headdim: int = 64
    ssd_groups: int = 8
    ssd_state: int = 128
    conv_w: int = 4
    s5_group: int = 16
    s5_state: int = 64
    n_experts: int = 32
    top_k: int = 4
    d_ff: int = 4096
    ssd_chunk: int = 128
    s5_rows: int = 512
    moe_rows: int = 1280
    moe_unit: int = 128
    moe_gather: int = 64
    moe_tf: int = 256
    moe_td: int = 256
    mm_tm: int = 1088
    mm_tn: int = 640

    @property
    def heads(self):
        return self.d_ssd // self.headdim

    @property
    def gn(self):
        return self.ssd_groups * self.ssd_state

    @property
    def conv_dim(self):
        return self.d_ssd + 2 * self.gn

    @property
    def s5_groups(self):
        return self.d_s5 // self.s5_group

    @property
    def tp(self):
        return self.batch * self.seq

    @property
    def ts(self):
        return self.dec_batch * self.dec_seq

    @property
    def t(self):
        return self.tp + self.ts

    @property
    def proj_w(self):
        return 2 * self.d_ssd + 2 * self.gn + self.d_s5 + LANES


def _pick(n, cap, mult):
    best = None
    for d in range(mult, min(n, cap) + 1, mult):
        if n % d == 0:
            best = d
    assert best is not None, (n, cap, mult)
    return best


def _cparams(sem, **kw):
    return pltpu.CompilerParams(dimension_semantics=sem, vmem_limit_bytes=VMEM_LIMIT_MB * 1024 * 1024, **kw)


def _sigmoid(x):
    return 1.0 / (1.0 + jnp.exp(-x))


def _split3(x):
    h = x.astype(BF16)
    r = x - h.astype(F32)
    m = r.astype(BF16)
    lo = (r - m.astype(F32)).astype(BF16)
    return h, m, lo


def _dot(a, b):
    return jnp.dot(a, b, preferred_element_type=F32)


def _mod_block_index(cfg, i, tm):
    return jnp.where(i < cfg.tp // tm, (i * tm) // cfg.seq, cfg.batch)


def _ada_body(c_ref, w_ref, b_ref, o_ref, a_ref):
    @pl.when(jnp.logical_and(pl.program_id(0) == 0, pl.program_id(1) == 0))
    def _():
        c = c_ref[...]
        a_ref[...] = (c * _sigmoid(c)).astype(BF16)

    o_ref[...] = _dot(a_ref[...], w_ref[...].astype(BF16)) + b_ref[...]


def _ada_call(c_all, w_ada, b_ada):
    depth, d, n6 = w_ada.shape
    rc = c_all.shape[0]
    tn = _pick(n6, 1024, LANES)
    return pl.pallas_call(
        _ada_body,
        out_shape=jax.ShapeDtypeStruct((depth, rc, n6), F32),
        grid=(depth, n6 // tn),
        in_specs=[
            pl.BlockSpec((rc, d), lambda l, n: (0, 0)),
            pl.BlockSpec((None, d, tn), lambda l, n: (l, 0, n)),
            pl.BlockSpec((None, 1, tn), lambda l, n: (l, 0, n)),
        ],
        out_specs=pl.BlockSpec((None, rc, tn), lambda l, n: (l, 0, n)),
        scratch_shapes=[pltpu.VMEM((rc, d), BF16)],
        compiler_params=_cparams(("arbitrary", "arbitrary")),
        name="ada",
    )(c_all, w_ada, b_ada.reshape(depth, 1, n6))


def _rms_mod(x, g, sc, sh, nrep, nb):
    d = x.shape[-1]
    y = x * lax.rsqrt(jnp.mean(x * x, axis=-1, keepdims=True) + EPS) * g
    y3 = y.reshape(nrep, nb, d)
    h = y3 * (1.0 + sc[None]) + sh[None]
    return h.reshape(nrep * nb, d)


def _norm_mod_body(x_ref, g_ref, sh_ref, sc_ref, o_ref, *, nrep, nb):
    h = _rms_mod(x_ref[...], g_ref[...], sc_ref[...], sh_ref[...], nrep, nb)
    o_ref[...] = h.astype(o_ref.dtype)


def _norm_mod_call(cfg, x, g, mod_rows, layer, shift_chunk, scale_chunk):
    t, d = x.shape
    nb = cfg.dec_batch
    tm = cfg.ts
    bidx = functools.partial(_mod_block_index, cfg, tm=tm)
    return pl.pallas_call(
        functools.partial(_norm_mod_body, nrep=tm // nb, nb=nb),
        out_shape=jax.ShapeDtypeStruct((t, d), BF16),
        grid=(t // tm,),
        in_specs=[
            pl.BlockSpec((tm, d), lambda i: (i, 0)),
            pl.BlockSpec((1, d), lambda i: (0, 0)),
            pl.BlockSpec((None, nb, d), lambda i: (layer, bidx(i), shift_chunk)),
            pl.BlockSpec((None, nb, d), lambda i: (layer, bidx(i), scale_chunk)),
        ],
        out_specs=pl.BlockSpec((tm, d), lambda i: (i, 0)),
        compiler_params=_cparams(("arbitrary",)),
        name="norm_mod",
    )(x, g.reshape(1, d), mod_rows, mod_rows)


def _mm_body(a_ref, w_ref, o_ref, wbf_ref):
    @pl.when(pl.program_id(1) == 0)
    def _():
        wbf_ref[...] = w_ref[...].astype(BF16)

    o_ref[...] = _dot(a_ref[...], wbf_ref[...]).astype(o_ref.dtype)


def _mm_call(cfg, a, w):
    m, k = a.shape
    n = w.shape[1]
    tm = _pick(m, cfg.mm_tm, 16)
    tn = _pick(n, cfg.mm_tn, LANES)
    return pl.pallas_call(
        _mm_body,
        out_shape=jax.ShapeDtypeStruct((m, n), F32),
        grid=(n // tn, m // tm),
        in_specs=[
            pl.BlockSpec((tm, k), lambda j, i: (i, 0)),
            pl.BlockSpec((k, tn), lambda j, i: (0, j)),
        ],
        out_specs=pl.BlockSpec((tm, tn), lambda j, i: (i, j)),
        scratch_shapes=[pltpu.VMEM((k, tn), BF16)],
        compiler_params=_cparams(("arbitrary", "arbitrary")),
        name="in_proj",
    )(a, w)


def _ssd_body(*refs, cfg, q_in, l_real, has_init):
    if has_init:
        (dsk_ref, z_ref, xs_ref, bc_ref, dt_ref, cw_ref, cb_ref, dtb_ref, alog_ref, ng_ref,
         h0_ref, tail0_ref, y_ref, hout_ref, cout_ref, tail_ref, ext_ref, yt_ref) = refs
    else:
        (dsk_ref, z_ref, xs_ref, bc_ref, dt_ref, cw_ref, cb_ref, dtb_ref, alog_ref, ng_ref,
         y_ref, hout_ref, cout_ref, tail_ref, ext_ref, yt_ref) = refs
    q = cfg.ssd_chunk
    p = cfg.headdim
    n = cfg.ssd_state
    g_cnt = cfg.ssd_groups
    heads = cfg.heads
    hpg = heads // g_cnt
    d_ssd = cfg.d_ssd
    cdim = cfg.conv_dim
    c = pl.program_id(1)

    @pl.when(c == 0)
    def _():
        if has_init:
            hout_ref[...] = h0_ref[...]
            tail_ref[...] = jnp.concatenate(
                [jnp.zeros((SUBLANES - cfg.conv_w + 1, cdim), F32), tail0_ref[...]], axis=0)
        else:
            hout_ref[...] = jnp.zeros(hout_ref.shape, F32)
            tail_ref[...] = jnp.zeros(tail_ref.shape, F32)

    ext_ref[0:SUBLANES, :] = tail_ref[...]
    ext_ref[SUBLANES:SUBLANES + q_in, :] = jnp.concatenate([xs_ref[...], bc_ref[...]], axis=1)
    conv = jnp.broadcast_to(cb_ref[...], (q_in, cdim))
    for k in range(cfg.conv_w):
        conv = conv + ext_ref[pl.ds(SUBLANES - cfg.conv_w + 1 + k, q_in), :] * cw_ref[k:k + 1, :]
    tail_ref[...] = ext_ref[pl.ds(q_in, SUBLANES), :]
    cout_ref[...] = ext_ref[pl.ds(SUBLANES + l_real - (cfg.conv_w - 1), cfg.conv_w - 1), :]

    act = conv * _sigmoid(conv)
    dtv = dt_ref[...] + dtb_ref[...]
    dtv = jnp.maximum(dtv, 0.0) + jnp.log(1.0 + jnp.exp(-jnp.abs(dtv)))
    if l_real < q_in:
        rows = lax.broadcasted_iota(I32, (q_in, LANES), 0)
        dtv = jnp.where(rows < l_real, dtv, 0.0)
    if q_in < q:
        act = jnp.concatenate([act, jnp.zeros((q - q_in, cdim), F32)], axis=0)
        dtv = jnp.concatenate([dtv, jnp.zeros((q - q_in, LANES), F32)], axis=0)
    xs_a = act[:, :d_ssd]
    bm = act[:, d_ssd:d_ssd + cfg.gn]
    cm = act[:, d_ssd + cfg.gn:]
    da = dtv * (-jnp.exp(alog_ref[...]))
    ri = lax.broadcasted_iota(I32, (q, q), 0)
    ci = lax.broadcasted_iota(I32, (q, q), 1)
    tril = (ri >= ci).astype(BF16)
    d_h, d_m, d_l = _split3(da)
    cs = _dot(tril, d_h) + _dot(tril, d_m) + _dot(tril, d_l)
    cs_t = cs.T
    dt_t = dtv.T
    xs_t = xs_a.T
    causal_t = ci >= ri

    for g in range(g_cnt):
        bm_g = bm[:, g * n:(g + 1) * n].astype(BF16)
        cm_t = cm[:, g * n:(g + 1) * n].T.astype(BF16)
        cb_t = _dot(bm_g, cm_t)
        for hh in range(hpg):
            h = g * hpg + hh
            cs_col = cs[:, h:h + 1]
            cs_row = cs_t[h:h + 1, :]
            l_t = cb_t * jnp.exp(jnp.where(causal_t, cs_row - cs_col, -jnp.inf))
            x_t = xs_t[h * p:(h + 1) * p, :]
            xdt_t = x_t * dt_t[h:h + 1, :]
            y_diag = _dot(xdt_t.astype(BF16), l_t.astype(BF16))
            hs = hout_ref[h]
            y_off = _dot(hs.astype(BF16), cm_t) * jnp.exp(cs_row)
            cs_last = cs_row[:, q - 1:q]
            xw_t = xdt_t * jnp.exp(cs_last - cs_row)
            hout_ref[h] = jnp.exp(cs_last) * hs + _dot(xw_t.astype(BF16), bm_g)
            yt_ref[h * p:(h + 1) * p, :] = y_diag + y_off + dsk_ref[h] * x_t

    y = yt_ref[...].T
    z = z_ref[...]
    if q_in < q:
        y = y[:q_in]
    yz = y * (z * _sigmoid(z))
    gw = d_ssd // g_cnt
    parts = []
    for g in range(g_cnt):
        seg = yz[:, g * gw:(g + 1) * gw]
        parts.append(seg * lax.rsqrt(jnp.mean(seg * seg, axis=-1, keepdims=True) + EPS))
    y_ref[...] = (jnp.concatenate(parts, axis=1) * ng_ref[...]).astype(y_ref.dtype)


def _ssd_call(cfg, proj, params, init=None, layer=0):
    q = cfg.ssd_chunk
    d_ssd, gn2, cdim = cfg.d_ssd, 2 * cfg.gn, cfg.conv_dim
    heads, p, n = cfg.heads, cfg.headdim, cfg.ssd_state
    assert (2 * d_ssd) % gn2 == 0 and heads <= LANES
    bc_blk = (2 * d_ssd) // gn2
    dt_blk = (2 * d_ssd + gn2 + cfg.d_s5) // LANES
    conv_w, conv_b, dt_bias, a_log, d_skip, norm_g = params
    pad = LANES - heads
    small = [
        conv_w, conv_b.reshape(1, cdim),
        jnp.pad(dt_bias, (0, pad)).reshape(1, LANES), jnp.pad(a_log, (0, pad)).reshape(1, LANES),
        norm_g.reshape(1, d_ssd),
    ]
    small_specs = [
        pl.BlockSpec((cfg.conv_w, cdim), lambda b, c: (0, 0)),
        pl.BlockSpec((1, cdim), lambda b, c: (0, 0)),
        pl.BlockSpec((1, LANES), lambda b, c: (0, 0)),
        pl.BlockSpec((1, LANES), lambda b, c: (0, 0)),
        pl.BlockSpec((1, d_ssd), lambda b, c: (0, 0)),
    ]
    smem = pl.BlockSpec(memory_space=pltpu.SMEM)
    if init is None:
        nb, nc, q_in, l_real = cfg.batch, cfg.seq // q, q, q
        row_specs = [
            pl.BlockSpec((q, d_ssd), lambda b, c: (b * nc + c, 0)),
            pl.BlockSpec((q, d_ssd), lambda b, c: (b * nc + c, 1)),
            pl.BlockSpec((q, gn2), lambda b, c: (b * nc + c, bc_blk)),
            pl.BlockSpec((q, LANES), lambda b, c: (b * nc + c, dt_blk)),
        ]
        init_args, init_specs = [], []
        y_shape = jax.ShapeDtypeStruct((cfg.tp, d_ssd), BF16)
        y_spec = pl.BlockSpec((q, d_ssd), lambda b, c: (b * nc + c, 0))
    else:
        nb, nc, q_in, l_real = cfg.dec_batch, 1, SUBLANES, cfg.dec_seq
        row_specs = [
            pl.BlockSpec((None, q_in, d_ssd), lambda b, c: (b, 0, 0)),
            pl.BlockSpec((None, q_in, d_ssd), lambda b, c: (b, 0, 1)),
            pl.BlockSpec((None, q_in, gn2), lambda b, c: (b, 0, bc_blk)),
            pl.BlockSpec((None, q_in, LANES), lambda b, c: (b, 0, dt_blk)),
        ]
        init_args = list(init)
        init_specs = [
            pl.BlockSpec((None, None, heads, p, n), lambda b, c: (layer, b, 0, 0, 0)),
            pl.BlockSpec((None, None, cfg.conv_w - 1, cdim), lambda b, c: (layer, b, 0, 0)),
        ]
        y_shape = jax.ShapeDtypeStruct((nb, q_in, d_ssd), BF16)
        y_spec = pl.BlockSpec((None, q_in, d_ssd), lambda b, c: (b, 0, 0))
    return pl.pallas_call(
        functools.partial(_ssd_body, cfg=cfg, q_in=q_in, l_real=l_real, has_init=init is not None),
        out_shape=[
            y_shape,
            jax.ShapeDtypeStruct((nb, heads, p, n), F32),
            jax.ShapeDtypeStruct((nb, cfg.conv_w - 1, cdim), F32),
        ],
        grid=(nb, nc),
        in_specs=[smem] + row_specs + small_specs + init_specs,
        out_specs=[
            y_spec,
            pl.BlockSpec((None, heads, p, n), lambda b, c: (b, 0, 0, 0)),
            pl.BlockSpec((None, cfg.conv_w - 1, cdim), lambda b, c: (b, 0, 0)),
        ],
        scratch_shapes=[
            pltpu.VMEM((SUBLANES, cdim), F32),
            pltpu.VMEM((q + 2 * SUBLANES, cdim), F32),
            pltpu.VMEM((d_ssd, q), F32),
        ],
        compiler_params=_cparams(("arbitrary", "arbitrary")),
        name="ssd_sample" if init is not None else "ssd_prompt",
    )(d_skip, proj, proj, proj, proj, *small, *init_args)


S5_TILE_GROUPS = LANES // 16


def _s5_tables(cfg, lam_re, lam_im, log_step, b_re, b_im, c_re, c_im):
    g_cnt, n, k = cfg.s5_groups, cfg.s5_state, cfg.s5_group
    tg = LANES // k
    n_lt = g_cnt // tg
    depth = lam_re.shape[0]
    lam = lax.complex(lam_re, lam_im)
    delta = jnp.exp(log_step)[..., None]
    lam_bar = jnp.exp(lam * delta)
    b_bar = ((lam_bar - 1.0) / lam)[..., None] * lax.complex(b_re, b_im)
    eye = jnp.eye(tg, dtype=F32)

    def blockdiag(m):
        a, b = m.shape[2], m.shape[3]
        m5 = m.reshape(depth, n_lt, tg, a, b)
        return (m5[:, :, :, :, None, :] * eye[None, None, :, None, :, None]).reshape(depth, n_lt, tg * a, tg * b)

    bt = jnp.swapaxes(b_bar, 2, 3)
    bdb = jnp.concatenate([blockdiag(jnp.real(bt)), blockdiag(jnp.imag(bt))], axis=-1).astype(BF16)
    ct_re = jnp.swapaxes(c_re, 2, 3)
    ct_im = jnp.swapaxes(c_im, 2, 3)
    bdc = jnp.concatenate([blockdiag(ct_re), blockdiag(-ct_im)], axis=2).astype(BF16)
    pows = []
    cur = lam_bar
    for _ in range(SUBLANES):
        pows.append(cur.reshape(depth, n_lt, tg * n))
        cur = cur * lam_bar
    pwc = jnp.stack(pows, axis=2)
    pw = jnp.concatenate([jnp.real(pwc), jnp.imag(pwc)], axis=-1)
    return bdb, bdc, pw


def _gelu(y):
    return 0.5 * y * (1.0 + jnp.tanh(0.7978845608028654 * (y + 0.044715 * y * y * y)))


def _s5_prompt_body(u_ref, bdb_ref, bdc_ref, pw_ref, d_ref, y_ref, sre_ref, sim_ref, car_ref, *, rows, half):
    t = pl.program_id(2)

    @pl.when(t == 0)
    def _():
        car_ref[...] = jnp.zeros(car_ref.shape, F32)

    u = u_ref[...]
    bu = _dot(u.astype(BF16), bdb_ref[...])
    xr = bu[:, :half]
    xi = bu[:, half:]
    sub = jnp.bitwise_and(lax.broadcasted_iota(I32, (rows, half), 0), SUBLANES - 1)
    step = 1
    while step < SUBLANES:
        pr = pw_ref[step - 1:step, :half]
        pi = pw_ref[step - 1:step, half:]
        keep = sub >= step
        sr = jnp.where(keep, pltpu.roll(xr, step, 0), 0.0)
        si = jnp.where(keep, pltpu.roll(xi, step, 0), 0.0)
        xr, xi = xr + pr * sr - pi * si, xi + pr * si + pi * sr
        step *= 2
    pr8 = pw_ref[:, :half]
    pi8 = pw_ref[:, half:]
    cr = car_ref[0:1, :half]
    cim = car_ref[0:1, half:]
    out_r, out_i = [], []
    for v in range(rows // SUBLANES):
        crb = jnp.broadcast_to(cr, (SUBLANES, half))
        cib = jnp.broadcast_to(cim, (SUBLANES, half))
        gr = xr[v * SUBLANES:(v + 1) * SUBLANES, :] + (pr8 * crb - pi8 * cib)
        gi = xi[v * SUBLANES:(v + 1) * SUBLANES, :] + (pr8 * cib + pi8 * crb)
        cr = gr[SUBLANES - 1:SUBLANES, :]
        cim = gi[SUBLANES - 1:SUBLANES, :]
        out_r.append(gr)
        out_i.append(gi)
    xr = jnp.concatenate(out_r, axis=0)
    xi = jnp.concatenate(out_i, axis=0)
    car_ref[0:1, :half] = xr[rows - 1:rows, :]
    car_ref[0:1, half:] = xi[rows - 1:rows, :]
    sre_ref[...] = xr[rows - 1:rows, :]
    sim_ref[...] = xi[rows - 1:rows, :]
    s_cat = jnp.concatenate([xr, xi], axis=1).astype(BF16)
    y = _dot(s_cat, bdc_ref[...]) + d_ref[...] * u
    y_ref[...] = _gelu(y)


def _s5_prompt_call(cfg, proj, tables, d_skip, layer):
    bdb, bdc, pw = tables
    n_lt = bdb.shape[1]
    half = bdb.shape[3] // 2
    rows = cfg.s5_rows
    assert rows % SUBLANES == 0 and cfg.seq % rows == 0
    nt = cfg.seq // rows
    u_blk = (2 * cfg.d_ssd + 2 * cfg.gn) // LANES
    return pl.pallas_call(
        functools.partial(_s5_prompt_body, rows=rows, half=half),
        out_shape=[
            jax.ShapeDtypeStruct((cfg.tp, cfg.d_s5), F32),
            jax.ShapeDtypeStruct((cfg.batch, n_lt, 1, half), F32),
            jax.ShapeDtypeStruct((cfg.batch, n_lt, 1, half), F32),
        ],
        grid=(cfg.batch, n_lt, nt),
        in_specs=[
            pl.BlockSpec((rows, LANES), lambda b, lt, t: (b * nt + t, u_blk + lt)),
            pl.BlockSpec((None, None, LANES, 2 * half), lambda b, lt, t: (layer, lt, 0, 0)),
            pl.BlockSpec((None, None, 2 * half, LANES), lambda b, lt, t: (layer, lt, 0, 0)),
            pl.BlockSpec((None, None, SUBLANES, 2 * half), lambda b, lt, t: (layer, lt, 0, 0)),
            pl.BlockSpec((1, LANES), lambda b, lt, t: (0, lt)),
        ],
        out_specs=[
            pl.BlockSpec((rows, LANES), lambda b, lt, t: (b * nt + t, lt)),
            pl.BlockSpec((None, None, 1, half), lambda b, lt, t: (b, lt, 0, 0)),
            pl.BlockSpec((None, None, 1, half), lambda b, lt, t: (b, lt, 0, 0)),
        ],
        scratch_shapes=[pltpu.VMEM((SUBLANES, 2 * half), F32)],
        compiler_params=_cparams(("arbitrary", "arbitrary", "arbitrary")),
        name="s5_prompt",
    )(proj, bdb, bdc, pw, d_skip.reshape(1, cfg.d_s5))


def _s5_sample_body(u_ref, bdb_ref, bdc_ref, pw_ref, d_ref, s0r_ref, s0i_ref, y_ref, sre_ref, sim_ref,
                    *, nb, nseq, half):
    u = u_ref[...]
    bu = _dot(u.astype(BF16), bdb_ref[...])
    lr = pw_ref[0:1, :half]
    li = pw_ref[0:1, half:]
    sr = s0r_ref[...]
    si = s0i_ref[...]
    st_r, st_i = [], []
    for s in range(nseq):
        br = bu[s * nb:(s + 1) * nb, :half]
        bi = bu[s * nb:(s + 1) * nb, half:]
        sr, si = lr * sr - li * si + br, lr * si + li * sr + bi
        st_r.append(sr)
        st_i.append(si)
    sre_ref[...] = sr
    sim_ref[...] = si
    s_cat = jnp.concatenate([jnp.concatenate(st_r, axis=0), jnp.concatenate(st_i, axis=0)], axis=1)
    y = _dot(s_cat.astype(BF16), bdc_ref[...]) + d_ref[...] * u
    y_ref[...] = _gelu(y)


def _s5_sample_call(cfg, proj, tables, d_skip, s_re0, s_im0, layer):
    bdb, bdc, pw = tables
    n_lt = bdb.shape[1]
    half = bdb.shape[3] // 2
    nb, nseq = cfg.dec_batch, cfg.dec_seq
    assert cfg.tp % cfg.ts == 0
    u_blk = (2 * cfg.d_ssd + 2 * cfg.gn) // LANES
    row_blk = cfg.tp // cfg.ts
    gsz = cfg.s5_groups * cfg.s5_state
    return pl.pallas_call(
        functools.partial(_s5_sample_body, nb=nb, nseq=nseq, half=half),
        out_shape=[
            jax.ShapeDtypeStruct((cfg.ts, cfg.d_s5), F32),
            jax.ShapeDtypeStruct((nb, gsz), F32),
            jax.ShapeDtypeStruct((nb, gsz), F32),
        ],
        grid=(n_lt,),
        in_specs=[
            pl.BlockSpec((cfg.ts, LANES), lambda lt: (row_blk, u_blk + lt)),
            pl.BlockSpec((None, None, LANES, 2 * half), lambda lt: (layer, lt, 0, 0)),
            pl.BlockSpec((None, None, 2 * half, LANES), lambda lt: (layer, lt, 0, 0)),
            pl.BlockSpec((None, None, SUBLANES, 2 * half), lambda lt: (layer, lt, 0, 0)),
            pl.BlockSpec((1, LANES), lambda lt: (0, lt)),
            pl.BlockSpec((None, nb, half), lambda lt: (layer, 0, lt)),
            pl.BlockSpec((None, nb, half), lambda lt: (layer, 0, lt)),
        ],
        out_specs=[
            pl.BlockSpec((cfg.ts, LANES), lambda lt: (0, lt)),
            pl.BlockSpec((nb, half), lambda lt: (0, lt)),
            pl.BlockSpec((nb, half), lambda lt: (0, lt)),
        ],
        compiler_params=_cparams(("arbitrary",)),
        name="s5_sample",
    )(proj, bdb, bdc, pw, d_skip.reshape(1, cfg.d_s5), s_re0.reshape(-1, nb, gsz), s_im0.reshape(-1, nb, gsz))


def _glu_norm_body(y_ref, w_ref, g_ref, o_ref, wbf_ref):
    @pl.when(pl.program_id(0) == 0)
    def _():
        wbf_ref[...] = w_ref[...].astype(BF16)

    y = y_ref[...]
    v = y * _sigmoid(_dot(y.astype(BF16), wbf_ref[...]))
    v = v * lax.rsqrt(jnp.mean(v * v, axis=-1, keepdims=True) + EPS) * g_ref[...]
    o_ref[...] = v.astype(o_ref.dtype)


def _glu_norm_call(cfg, y, w_glu, g, layer):
    t, d = y.shape
    tm = _pick(t, cfg.mm_tm, 16)
    return pl.pallas_call(
        _glu_norm_body,
        out_shape=jax.ShapeDtypeStruct((t, d), BF16),
        grid=(t // tm,),
        in_specs=[
            pl.BlockSpec((tm, d), lambda i: (i, 0)),
            pl.BlockSpec((None, d, d), lambda i: (layer, 0, 0)),
            pl.BlockSpec((1, d), lambda i: (0, 0)),
        ],
        out_specs=pl.BlockSpec((tm, d), lambda i: (i, 0)),
        scratch_shapes=[pltpu.VMEM((d, d), BF16)],
        compiler_params=_cparams(("arbitrary",)),
        name="s5_glu_norm",
    )(y, w_glu, g.reshape(1, d))


def _outproj_body(ya_ref, yb_ref, w_ref, x_ref, g1_ref, o_ref, wbf_ref, *, d_a, nrep, nb):
    @pl.when(pl.program_id(1) == 0)
    def _():
        wbf_ref[...] = w_ref[...].astype(BF16)

    mix = _dot(ya_ref[...], wbf_ref[:d_a, :]) + _dot(yb_ref[...], wbf_ref[d_a:, :])
    tn = mix.shape[1]
    gated = (mix.reshape(nrep, nb, tn) * g1_ref[...][None]).reshape(nrep * nb, tn)
    o_ref[...] = x_ref[...] + gated


def _outproj_call(cfg, ya, yb, w_out, x, mod_rows, layer, gate_chunk):
    t, d = x.shape
    d_a, d_b = ya.shape[1], yb.shape[1]
    nb = cfg.dec_batch
    tm = cfg.ts
    tn = _pick(d, 512, LANES)
    bidx = functools.partial(_mod_block_index, cfg, tm=tm)
    nblk = d // tn
    return pl.pallas_call(
        functools.partial(_outproj_body, d_a=d_a, nrep=tm // nb, nb=nb),
        out_shape=jax.ShapeDtypeStruct((t, d), F32),
        grid=(nblk, t // tm),
        in_specs=[
            pl.BlockSpec((tm, d_a), lambda j, i: (i, 0)),
            pl.BlockSpec((tm, d_b), lambda j, i: (i, 0)),
            pl.BlockSpec((None, d_a + d_b, tn), lambda j, i: (layer, 0, j)),
            pl.BlockSpec((tm, tn), lambda j, i: (i, j)),
            pl.BlockSpec((None, nb, tn), lambda j, i: (layer, bidx(i), gate_chunk * nblk + j)),
        ],
        out_specs=pl.BlockSpec((tm, tn), lambda j, i: (i, j)),
        scratch_shapes=[pltpu.VMEM((d_a + d_b, tn), BF16)],
        compiler_params=_cparams(("arbitrary", "arbitrary")),
        name="out_proj",
    )(ya, yb, w_out, x, mod_rows)


def _router_body(x_ref, g_ref, sh_ref, sc_ref, wr_ref, br_ref, h_ref, mi_ref, mf_ref, cnt_ref,
                 whi_ref, wlo_ref, car_ref, *, nrep, nb, top_k):
    i = pl.program_id(0)

    @pl.when(i == 0)
    def _():
        w = wr_ref[...]
        hi = w.astype(BF16)
        whi_ref[...] = hi
        wlo_ref[...] = (w - hi.astype(F32)).astype(BF16)
        car_ref[...] = jnp.zeros(car_ref.shape, F32)

    h = _rms_mod(x_ref[...], g_ref[...], sc_ref[...], sh_ref[...], nrep, nb)
    h_ref[...] = h
    tm = h.shape[0]
    hi = h.astype(BF16)
    lo = (h - hi.astype(F32)).astype(BF16)
    logits = _dot(hi, whi_ref[...]) + (_dot(hi, wlo_ref[...]) + _dot(lo, whi_ref[...])) + br_ref[...]
    lane_i = lax.broadcasted_iota(I32, (tm, LANES), 1)
    lane = lane_i.astype(F32)
    vals = logits
    idxs, tops = [], []
    for _ in range(top_k):
        m = jnp.max(vals, axis=-1, keepdims=True)
        idx = jnp.min(jnp.where(vals == m, lane, float(LANES)), axis=-1, keepdims=True)
        idxs.append(idx)
        tops.append(m)
        vals = jnp.where(lane == idx, -jnp.inf, vals)
    exps = [jnp.exp(v - tops[0]) for v in tops]
    denom = exps[0]
    for e in exps[1:]:
        denom = denom + e
    onehot = jnp.zeros((tm, LANES), F32)
    for idx in idxs:
        onehot = onehot + (lane == idx).astype(F32)
    ri = lax.broadcasted_iota(I32, (tm, tm), 0)
    ci = lax.broadcasted_iota(I32, (tm, tm), 1)
    before = _dot((ri > ci).astype(BF16), onehot.astype(BF16)) + car_ref[0:1, :]
    car_ref[0:1, :] = car_ref[0:1, :] + jnp.sum(onehot, axis=0, keepdims=True)
    cnt_ref[...] = car_ref[0:1, :]
    mi = jnp.zeros((tm, LANES), I32)
    mf = jnp.zeros((tm, LANES), F32)
    for k in range(top_k):
        rank = jnp.sum(jnp.where(lane == idxs[k], before, 0.0), axis=-1, keepdims=True).astype(I32)
        mi = jnp.where(lane_i == k, idxs[k].astype(I32), mi)
        mi = jnp.where(lane_i == top_k + k, rank, mi)
        mf = jnp.where(lane_i == k, exps[k] / denom, mf)
    mi_ref[...] = mi
    mf_ref[...] = mf


def _router_call(cfg, x, g, mod_rows, layer, shift_chunk, scale_chunk, w_router, b_router):
    t, d = x.shape
    nb = cfg.dec_batch
    tm = cfg.ts
    e = cfg.n_experts
    assert e <= LANES and 2 * cfg.top_k <= LANES
    bidx = functools.partial(_mod_block_index, cfg, tm=tm)
    wr = jnp.pad(w_router, ((0, 0), (0, LANES - e)))
    br = jnp.pad(b_router, (0, LANES - e), constant_values=NEG_BIG).reshape(1, LANES)
    return pl.pallas_call(
        functools.partial(_router_body, nrep=tm // nb, nb=nb, top_k=cfg.top_k),
        out_shape=[
            jax.ShapeDtypeStruct((t, d), F32),
            jax.ShapeDtypeStruct((t, LANES), I32),
            jax.ShapeDtypeStruct((t, LANES), F32),
            jax.ShapeDtypeStruct((1, LANES), F32),
        ],
        grid=(t // tm,),
        in_specs=[
            pl.BlockSpec((tm, d), lambda i: (i, 0)),
            pl.BlockSpec((1, d), lambda i: (0, 0)),
            pl.BlockSpec((None, nb, d), lambda i: (layer, bidx(i), shift_chunk)),
            pl.BlockSpec((None, nb, d), lambda i: (layer, bidx(i), scale_chunk)),
            pl.BlockSpec((d, LANES), lambda i: (0, 0)),
            pl.BlockSpec((1, LANES), lambda i: (0, 0)),
        ],
        out_specs=[
            pl.BlockSpec((tm, d), lambda i: (i, 0)),
            pl.BlockSpec((tm, LANES), lambda i: (i, 0)),
            pl.BlockSpec((tm, LANES), lambda i: (i, 0)),
            pl.BlockSpec((1, LANES), lambda i: (0, 0)),
        ],
        scratch_shapes=[
            pltpu.VMEM((d, LANES), BF16),
            pltpu.VMEM((d, LANES), BF16),
            pltpu.VMEM((SUBLANES, LANES), F32),
        ],
        compiler_params=_cparams(("arbitrary",)),
        name="router",
    )(x, g.reshape(1, d), mod_rows, mod_rows, wr, br)


def _moe_body(ie_ref, nu_ref, nit_ref, tok_ref, h_hbm, wg_ref, wu_ref, wd_ref, bg_ref, bu_ref, bd_ref,
              o_ref, stage_ref, xb_ref, act_ref, sem, *, cfg, n_f):
    i = pl.program_id(0)
    j = pl.program_id(1)
    unit, gb = cfg.moe_unit, cfg.moe_gather
    td = cfg.moe_td

    @pl.when(i < nit_ref[0])
    def _():
        nu = nu_ref[i]
        n_big = lax.shift_right_logical(nu, 2)
        rem = nu - 4 * n_big
        base = n_big * (4 * unit)

        def for_blocks(fn):
            def big(s, carry):
                fn(pl.multiple_of(s * (4 * unit), 4 * unit), 4 * unit)
                return carry
            lax.fori_loop(0, n_big, big, 0)

            @pl.when(rem >= 2)
            def _():
                fn(pl.multiple_of(base, 2 * unit), 2 * unit)

            @pl.when(jnp.bitwise_and(rem, 1) == 1)
            def _():
                fn(pl.multiple_of(base + lax.shift_right_logical(rem, 1) * (2 * unit), unit), unit)

        @pl.when(jnp.logical_and(j == 0, nu > 0))
        def _():
            n_blk = nu * (unit // gb)

            def issue(blk, slot):
                def one(r, carry):
                    tok = tok_ref[0, blk * gb + r]
                    pltpu.make_async_copy(h_hbm.at[pl.ds(tok, 1), :], stage_ref.at[slot, pl.ds(r, 1), :],
                                          sem.at[slot]).start()
                    return carry
                lax.fori_loop(0, gb, one, 0, unroll=8)

            issue(0, 0)

            def blk_step(blk, carry):
                slot = blk % 2

                @pl.when(blk + 1 < n_blk)
                def _():
                    issue(blk + 1, 1 - slot)

                pltpu.make_async_copy(h_hbm.at[pl.ds(0, gb), :], stage_ref.at[slot], sem.at[slot]).wait()
                xb_ref[pl.ds(pl.multiple_of(blk * gb, gb), gb), :] = stage_ref[slot].astype(BF16)
                return carry

            lax.fori_loop(0, n_blk, blk_step, 0)

        @pl.when(j < n_f)
        def _():
            def blk(r0, size):
                x = xb_ref[pl.ds(r0, size), :]
                g = _dot(x, wg_ref[...].astype(BF16)) + bg_ref[...]
                u = _dot(x, wu_ref[...].astype(BF16)) + bu_ref[...]
                gl = jnp.minimum(g, SWIGLU_LIMIT)
                ul = jnp.clip(u, -SWIGLU_LIMIT, SWIGLU_LIMIT)
                a = gl * _sigmoid(SWIGLU_ALPHA * gl) * (ul + 1.0)
                act_ref[j, pl.ds(r0, size), :] = a.astype(BF16)

            for_blocks(blk)

        @pl.when(j >= n_f)
        def _():
            def blk(r0, size):
                a = jnp.concatenate([act_ref[jj, pl.ds(r0, size), :] for jj in range(n_f)], axis=1)
                o_ref[pl.ds(r0, size), :] = _dot(a, wd_ref[...].astype(BF16)) + bd_ref[...]

            for_blocks(blk)

            def clear(s, carry):
                o_ref[pl.ds(pl.multiple_of(s * unit, unit), unit), :] = jnp.zeros((unit, td), F32)
                return carry

            lax.fori_loop(nu, cfg.moe_rows // unit, clear, 0)

    @pl.when(jnp.logical_and(i >= nit_ref[0], j >= n_f))
    def _():
        o_ref[...] = jnp.zeros(o_ref.shape, F32)


def _moe_call(cfg, layer, h2, tok_tbl, item_e, item_nu, n_items, w_gate_up, b_gate_up, w_down, b_down):
    t, d = h2.shape
    tm, tf, td, d_ff = cfg.moe_rows, cfg.moe_tf, cfg.moe_td, cfg.d_ff
    ni = tok_tbl.shape[0]
    n_f, n_d = d_ff // tf, d // td
    depth, e = w_gate_up.shape[0], w_gate_up.shape[1]
    steps = n_f + n_d
    assert tm % cfg.moe_unit == 0 and cfg.moe_unit % cfg.moe_gather == 0

    def eff(i, j, nit):
        valid = i < nit[0]
        return jnp.where(valid, i, nit[0] - 1), jnp.where(valid, j, steps - 1)

    def wg_map(i, j, ie, ns, nit):
        i2, j2 = eff(i, j, nit)
        return (layer, ie[i2], 0, jnp.minimum(j2, n_f - 1))

    def wu_map(i, j, ie, ns, nit):
        i2, j2 = eff(i, j, nit)
        return (layer, ie[i2], 0, n_f + jnp.minimum(j2, n_f - 1))

    def wd_map(i, j, ie, ns, nit):
        i2, j2 = eff(i, j, nit)
        return (layer, ie[i2], 0, jnp.maximum(j2 - n_f, 0))

    def o_map(i, j, ie, ns, nit):
        return (i, jnp.maximum(j - n_f, 0))

    grid_spec = pltpu.PrefetchScalarGridSpec(
        num_scalar_prefetch=3,
        grid=(ni, steps),
        in_specs=[
            pl.BlockSpec((None, 1, tm), lambda i, j, ie, ns, nit: (jnp.minimum(i, nit[0] - 1), 0, 0),
                         memory_space=pltpu.SMEM),
            pl.BlockSpec(memory_space=pl.ANY),
            pl.BlockSpec((None, None, d, tf), wg_map),
            pl.BlockSpec((None, None, d, tf), wu_map),
            pl.BlockSpec((None, None, d_ff, td), wd_map),
            pl.BlockSpec((None, None, 1, tf), wg_map),
            pl.BlockSpec((None, None, 1, tf), wu_map),
            pl.BlockSpec((None, None, 1, td), wd_map),
        ],
        out_specs=pl.BlockSpec((tm, td), o_map),
        scratch_shapes=[
            pltpu.VMEM((2, cfg.moe_gather, d), F32),
            pltpu.VMEM((tm, d), BF16),
            pltpu.VMEM((n_f, tm, tf), BF16),
            pltpu.SemaphoreType.DMA((2,)),
        ],
    )
    return pl.pallas_call(
        functools.partial(_moe_body, cfg=cfg, n_f=n_f),
        out_shape=jax.ShapeDtypeStruct((ni * tm, d), F32),
        grid_spec=grid_spec,
        compiler_params=_cparams(("arbitrary", "arbitrary"), disable_bounds_checks=True),
        name="moe_experts",
    )(item_e, item_nu, n_items, tok_tbl, h2, w_gate_up, w_gate_up, w_down,
      b_gate_up.reshape(depth, e, 1, 2 * d_ff), b_gate_up.reshape(depth, e, 1, 2 * d_ff),
      b_down.reshape(depth, e, 1, d))


def _moe_plan(cfg, idx, rank, counts):
    t, k = idx.shape
    tm, unit = cfg.moe_rows, cfg.moe_unit
    e = cfg.n_experts
    ni = e + (t * k) // tm
    items_per = (counts + tm - 1) // tm
    item_end = jnp.cumsum(items_per)
    item_start = item_end - items_per
    n_items = item_end[-1]
    per = ((counts + jnp.maximum(items_per, 1) - 1) // jnp.maximum(items_per, 1) + unit - 1) // unit * unit
    per = jnp.maximum(per, unit)
    ids = jnp.arange(ni, dtype=I32)
    item_e = jnp.minimum(jnp.searchsorted(item_end, ids, side="right"), e - 1).astype(I32)
    within = ids - item_start[item_e]
    rows = jnp.clip(counts[item_e] - within * per[item_e], 0, per[item_e])
    rows = jnp.where(ids < n_items, rows, 0)
    item_nu = ((rows + unit - 1) // unit).astype(I32)
    per_a = per[idx]
    dest = (item_start[idx] + rank // per_a) * tm + rank % per_a
    tok = jnp.broadcast_to(jnp.arange(t, dtype=I32)[:, None], (t, k))
    tok_tbl = jnp.zeros((ni * tm,), I32).at[dest.reshape(-1)].set(tok.reshape(-1), unique_indices=True)
    return tok_tbl.reshape(ni, 1, tm), item_e, item_nu, n_items.reshape(1).astype(I32), dest.astype(I32)


def _combine_body(dest_ref, ys_hbm, x_ref, gate_ref, g2_ref, o_ref, buf_ref, sem, *, top_k, rows):
    def one(r, carry):
        for k in range(top_k):
            slot = dest_ref[0, r * top_k + k]
            pltpu.make_async_copy(ys_hbm.at[pl.ds(slot, 1), :], buf_ref.at[k, pl.ds(r, 1), :], sem.at[0]).start()
        return carry

    lax.fori_loop(0, rows, one, 0, unroll=4)
    for k in range(top_k):
        pltpu.make_async_copy(ys_hbm.at[pl.ds(0, rows), :], buf_ref.at[k], sem.at[0]).wait()
    gates = gate_ref[...]
    acc = gates[:, 0:1] * buf_ref[0]
    for k in range(1, top_k):
        acc = acc + gates[:, k:k + 1] * buf_ref[k]
    o_ref[...] = x_ref[...] + g2_ref[...] * acc


def _combine_call(cfg, x, ys, gates, dest, mod_rows, layer, gate_chunk):
    t, d = x.shape
    tc = cfg.dec_batch
    k = cfg.top_k
    bidx = functools.partial(_mod_block_index, cfg, tm=tc)
    return pl.pallas_call(
        functools.partial(_combine_body, top_k=k, rows=tc),
        out_shape=jax.ShapeDtypeStruct((t, d), F32),
        grid=(t // tc,),
        in_specs=[
            pl.BlockSpec((None, 1, tc * k), lambda i: (i, 0, 0), memory_space=pltpu.SMEM),
            pl.BlockSpec(memory_space=pl.ANY),
            pl.BlockSpec((tc, d), lambda i: (i, 0)),
            pl.BlockSpec((tc, LANES), lambda i: (i, 0)),
            pl.BlockSpec((None, tc, d), lambda i: (layer, bidx(i), gate_chunk)),
        ],
        out_specs=pl.BlockSpec((tc, d), lambda i: (i, 0)),
        scratch_shapes=[pltpu.VMEM((k, tc, d), F32), pltpu.SemaphoreType.DMA((1,))],
        compiler_params=_cparams(("arbitrary",), disable_bounds_checks=True),
        name="moe_combine",
    )(dest.reshape(t // tc, 1, tc * k), ys, x, gates, mod_rows)


def _final_norm_body(x_ref, g_ref, op_ref, os_ref, *, n_prompt):
    i = pl.program_id(0)
    x = x_ref[...]
    y = x * lax.rsqrt(jnp.mean(x * x, axis=-1, keepdims=True) + EPS) * g_ref[...]

    @pl.when(i < n_prompt)
    def _():
        op_ref[...] = y

    @pl.when(i >= n_prompt)
    def _():
        os_ref[...] = y


def _final_norm_call(cfg, x, g):
    t, d = x.shape
    tm = cfg.ts
    n_prompt = cfg.tp // tm
    return pl.pallas_call(
        functools.partial(_final_norm_body, n_prompt=n_prompt),
        out_shape=[jax.ShapeDtypeStruct((cfg.tp, d), F32), jax.ShapeDtypeStruct((cfg.ts, d), F32)],
        grid=(t // tm,),
        in_specs=[pl.BlockSpec((tm, d), lambda i: (i, 0)), pl.BlockSpec((1, d), lambda i: (0, 0))],
        out_specs=[pl.BlockSpec((tm, d), lambda i: (jnp.minimum(i, n_prompt - 1), 0)),
                   pl.BlockSpec((tm, d), lambda i: (0, 0))],
        compiler_params=_cparams(("arbitrary",)),
        name="final_norm",
    )(x, g.reshape(1, d))


def _forward(cfg, x_prompt, x_sample, c_prompt, c_sample, state_s5_re, state_s5_im, state_ssd, state_conv, w):
    d = cfg.d_model
    nb, nseq = cfg.dec_batch, cfg.dec_seq
    assert cfg.seq % cfg.ts == 0 and cfg.seq % cfg.ssd_chunk == 0 and cfg.tp % cfg.ts == 0
    x = jnp.concatenate([x_prompt.reshape(cfg.tp, d),
                         jnp.swapaxes(x_sample, 0, 1).reshape(cfg.ts, d)], axis=0)
    c_all = jnp.concatenate([c_prompt, c_sample], axis=0)
    rc = c_all.shape[0]
    c_all = jnp.pad(c_all, ((0, (-rc) % 16), (0, 0)))
    mod = _ada_call(c_all, w["w_ada"], w["b_ada"])
    mod_rows = jnp.concatenate([jnp.repeat(mod[:, :cfg.batch], nb, axis=1), mod[:, cfg.batch:cfg.batch + nb]],
                               axis=1)
    o2 = cfg.d_ssd + cfg.conv_dim
    o3 = o2 + cfg.heads
    p_re, p_im, p_h, p_conv, s_re, s_im, s_h, s_conv = ([] for _ in range(8))
    tables = _s5_tables(cfg, w["s5_lambda_re"], w["s5_lambda_im"], w["s5_log_step"],
                        w["s5_b_re"], w["s5_b_im"], w["s5_c_re"], w["s5_c_im"])
    for l in range(cfg.depth):
        w_in = w["w_in"][l]
        w_in_r = jnp.concatenate([w_in[:, :o2], w_in[:, o3:], w_in[:, o2:o3],
                                  jnp.zeros((d, LANES - cfg.heads), F32)], axis=1)
        h = _norm_mod_call(cfg, x, w["norm1_g"][l], mod_rows, l, 0, 1)
        proj = _mm_call(cfg, h, w_in_r)
        ssd_params = (w["conv_w"][l], w["conv_b"][l], w["dt_bias"][l], w["a_log"][l], w["ssd_d"][l],
                      w["ssd_norm_g"][l])
        y_ssd_p, h_p, conv_p = _ssd_call(cfg, proj, ssd_params)
        proj_s = jnp.swapaxes(proj[cfg.tp:].reshape(nseq, nb, cfg.proj_w), 0, 1)
        proj_s = jnp.pad(proj_s, ((0, 0), (0, SUBLANES - nseq), (0, 0)))
        y_ssd_s, h_s, conv_s = _ssd_call(cfg, proj_s, ssd_params, init=(state_ssd, state_conv), layer=l)
        y_ssd = jnp.concatenate([y_ssd_p, jnp.swapaxes(y_ssd_s[:, :nseq], 0, 1).reshape(cfg.ts, cfg.d_ssd)],
                                axis=0)
        d_s5 = w["s5_d"][l].reshape(-1)
        yg_p, re_p, im_p = _s5_prompt_call(cfg, proj, tables, d_s5, l)
        yg_s, re_s, im_s = _s5_sample_call(cfg, proj, tables, d_s5, state_s5_re, state_s5_im, l)
        y_s5 = _glu_norm_call(cfg, jnp.concatenate([yg_p, yg_s], axis=0), w["s5_w_glu"], w["s5_norm_g"][l], l)
        x = _outproj_call(cfg, y_ssd, y_s5, w["w_out"], x, mod_rows, l, 2)
        h2, meta_i, meta_f, counts = _router_call(cfg, x, w["norm2_g"][l], mod_rows, l, 3, 4,
                                                  w["w_router"][l], w["b_router"][l])
        k = cfg.top_k
        plan = _moe_plan(cfg, meta_i[:, :k], meta_i[:, k:2 * k], counts[0, :cfg.n_experts].astype(I32))
        tok_tbl, item_e, item_nu, n_items, dest = plan
        ys = _moe_call(cfg, l, h2, tok_tbl, item_e, item_nu, n_items,
                       w["w_gate_up"], w["b_gate_up"], w["w_down"], w["b_down"])
        x = _combine_call(cfg, x, ys, meta_f, dest, mod_rows, l, 5)
        gshape = (cfg.s5_groups, cfg.s5_state)
        p_re.append(re_p.reshape((cfg.batch,) + gshape))
        p_im.append(im_p.reshape((cfg.batch,) + gshape))
        p_h.append(h_p)
        p_conv.append(conv_p)
        s_re.append(re_s.reshape((nb,) + gshape))
        s_im.append(im_s.reshape((nb,) + gshape))
        s_h.append(h_s)
        s_conv.append(conv_s)
    y_p, y_s = _final_norm_call(cfg, x, w["final_norm_g"])
    y_prompt = y_p.reshape(cfg.batch, cfg.seq, d)
    y_sample = jnp.swapaxes(y_s.reshape(nseq, nb, d), 0, 1)
    st = jnp.stack
    return (y_prompt, y_sample, st(p_re), st(p_im), st(p_h), st(p_conv), st(s_re), st(s_im), st(s_h), st(s_conv))


def kernel(x_prompt, x_sample, c_prompt, c_sample, state_s5_re, state_s5_im, state_ssd, state_conv,
           norm1_g, norm2_g, w_ada, b_ada, w_in, conv_w, conv_b, dt_bias, a_log, ssd_d, ssd_norm_g,
           s5_lambda_re, s5_lambda_im, s5_log_step, s5_b_re, s5_b_im, s5_c_re, s5_c_im, s5_d,
           s5_w_glu, s5_norm_g, w_out, w_router, b_router, w_gate_up, b_gate_up, w_down, b_down,
           final_norm_g):
    w = dict(norm1_g=norm1_g, norm2_g=norm2_g, w_ada=w_ada, b_ada=b_ada, w_in=w_in, conv_w=conv_w,
             conv_b=conv_b, dt_bias=dt_bias, a_log=a_log, ssd_d=ssd_d, ssd_norm_g=ssd_norm_g,
             s5_lambda_re=s5_lambda_re, s5_lambda_im=s5_lambda_im, s5_log_step=s5_log_step,
             s5_b_re=s5_b_re, s5_b_im=s5_b_im, s5_c_re=s5_c_re, s5_c_im=s5_c_im, s5_d=s5_d,
             s5_w_glu=s5_w_glu, s5_norm_g=s5_norm_g, w_out=w_out, w_router=w_router, b_router=b_router,
             w_gate_up=w_gate_up, b_gate_up=b_gate_up, w_down=w_down, b_down=b_down,
             final_norm_g=final_norm_g)
    return _forward(Cfg(), x_prompt, x_sample, c_prompt, c_sample, state_s5_re, state_s5_im, state_ssd,
                    state_conv, w)
```
